```python
import math
import jax
import jax.numpy as jnp
from jax import lax
import numpy as np

D_MODEL = 1024
BATCH = 8
SEQ = 4096
DEPTH = 1

HEAD_DIM = 64
SWA_Q_HEADS = 8
SWA_KV_HEADS = 2
SWA_GROUP = SWA_Q_HEADS // SWA_KV_HEADS
SB_HEADS = 8
WINDOW = 128
BLOCK = 128
ROPE_THETA = 10000.0
D_FF = 2816
CONV_WIDTH = 3
N_BRANCHES = 2
LN_EPS = 1e-5
DEEPNORM_ALPHA = (2.0 * DEPTH) ** 0.25
DEEPNORM_BETA = (8.0 * DEPTH) ** -0.25

SWA_Q_WIDTH = SWA_Q_HEADS * HEAD_DIM
SWA_KV_WIDTH = SWA_KV_HEADS * HEAD_DIM
SB_WIDTH = SB_HEADS * HEAD_DIM
GATE_WIDTH = N_BRANCHES * D_MODEL
IN_WIDTHS = [SWA_Q_WIDTH, SWA_KV_WIDTH, SWA_KV_WIDTH, SB_WIDTH, SB_WIDTH, SB_WIDTH, GATE_WIDTH]
IN_SPLITS = [int(v) for v in np.cumsum(IN_WIDTHS)[:-1]]
IN_TOTAL = int(sum(IN_WIDTHS))

kernel_name = 'hybrid_swa_sink_stickbreaking_convffn_deepnorm'


def layer_norm(x, g, b):
    xf = x.astype(jnp.float32)
    mu = jnp.mean(xf, axis=-1, keepdims=True)
    xc = xf - mu
    var = jnp.mean(xc * xc, axis=-1, keepdims=True)
    y = xc * lax.rsqrt(var + LN_EPS) * g.astype(jnp.float32) + b.astype(jnp.float32)
    return y.astype(x.dtype)


def rotary_tables(positions):
    inv_freq = 1.0 / (ROPE_THETA ** (jnp.arange(0, HEAD_DIM, 2, dtype=jnp.float32) / HEAD_DIM))
    ang = positions.astype(jnp.float32)[..., None] * inv_freq
    return jnp.cos(ang)[:, :, None, :], jnp.sin(ang)[:, :, None, :]


def apply_rope(t, cos, sin):
    tf = t.astype(jnp.float32)
    t1, t2 = jnp.split(tf, 2, axis=-1)
    out = jnp.concatenate([t1 * cos - t2 * sin, t2 * cos + t1 * sin], axis=-1)
    return out.astype(t.dtype)


def sliding_window_sink_attention(q, k, v, sinks):
    B, T, _, _ = q.shape
    n = T // BLOCK
    qb = q.reshape(B, n, BLOCK, SWA_KV_HEADS, SWA_GROUP, HEAD_DIM)
    pad = ((0, 0), (BLOCK, 0), (0, 0), (0, 0))
    kb = jnp.pad(k, pad).reshape(B, n + 1, BLOCK, SWA_KV_HEADS, HEAD_DIM)
    vb = jnp.pad(v, pad).reshape(B, n + 1, BLOCK, SWA_KV_HEADS, HEAD_DIM)
    kwin = jnp.concatenate([kb[:, :-1], kb[:, 1:]], axis=2)
    vwin = jnp.concatenate([vb[:, :-1], vb[:, 1:]], axis=2)
    scale = HEAD_DIM ** -0.5
    s = jnp.einsum('bnqhgd,bnshd->bnhgqs', qb, kwin).astype(jnp.float32) * scale
    blk = jnp.arange(n)[:, None, None]
    qloc = jnp.arange(BLOCK)[None, :, None] + BLOCK
    kloc = jnp.arange(2 * BLOCK)[None, None, :]
    rel = qloc - kloc
    kglob = blk * BLOCK + kloc - BLOCK
    mask = (rel >= 0) & (rel < WINDOW) & (kglob >= 0)
    s = jnp.where(mask[None, :, None, None], s, -jnp.inf)
    sink = sinks.astype(jnp.float32).reshape(SWA_KV_HEADS, SWA_GROUP)[None, None, :, :, None, None]
    m = jnp.maximum(jnp.max(s, axis=-1, keepdims=True), sink)
    p = jnp.exp(s - m)
    denom = jnp.sum(p, axis=-1, keepdims=True) + jnp.exp(sink - m)
    probs = (p / denom).astype(v.dtype)
    out = jnp.einsum('bnhgqs,bnshd->bnqhgd', probs, vwin)
    return out.reshape(B, T, SWA_Q_WIDTH)


def stick_breaking_attention(q, k, v):
    B, T, H, D = q.shape
    n = T // BLOCK
    qb = q.reshape(B, n, BLOCK, H, D).transpose(1, 0, 2, 3, 4)
    kpos = jnp.arange(T)
    scale = D ** -0.5

    def block(args):
        qi, i = args
        z = jnp.einsum('bqhd,bshd->bhqs', qi, k).astype(jnp.float32) * scale
        qpos = i * BLOCK + jnp.arange(BLOCK)
        mask = kpos[None, :] < qpos[:, None]
        log_beta = jax.nn.log_sigmoid(z)
        log_one_minus = jnp.where(mask, jax.nn.log_sigmoid(-z), 0.0)
        after = lax.cumsum(log_one_minus, axis=3, reverse=True) - log_one_minus
        a = jnp.where(mask, jnp.exp(log_beta + after), 0.0).astype(v.dtype)
        return jnp.einsum('bhqs,bshd->bqhd', a, v)

    out = lax.map(block, (qb, jnp.arange(n)))
    return out.transpose(1, 0, 2, 3, 4).reshape(B, T, H * D)


def causal_depthwise_conv(u, w, b):
    K, C = w.shape
    y = lax.conv_general_dilated(
        u, w[:, None, :].astype(u.dtype), window_strides=(1,), padding=[(K - 1, 0)],
        dimension_numbers=('NWC', 'WIO', 'NWC'), feature_group_count=C)
    return y + b


def conv_ffn(x, w_up, conv_w, conv_b, w_down):
    a = causal_depthwise_conv(x @ w_up, conv_w, conv_b)
    gate, up = jnp.split(a, 2, axis=-1)
    return (jax.nn.silu(gate) * up) @ w_down


def setup_inputs(seed: int = 0) -> dict:
    key = jax.random.key(seed)
    ks = jax.random.split(key, 20)
    f32 = jnp.float32

    def dense(k, fan_in, fan_out, scale=1.0):
        return jax.random.normal(k, (DEPTH, fan_in, fan_out), f32) * (fan_in ** -0.5) * scale

    x = jax.random.normal(ks[0], (BATCH, SEQ, D_MODEL), f32)
    offset = jax.random.randint(ks[1], (BATCH, 1), 0, 1024, dtype=jnp.int32)
    positions = (offset + jnp.arange(SEQ, dtype=jnp.int32)[None, :]).astype(jnp.int32)
    w_in = jnp.concatenate([
        dense(ks[2], D_MODEL, SWA_Q_WIDTH),
        dense(ks[3], D_MODEL, SWA_KV_WIDTH),
        dense(ks[4], D_MODEL, SWA_KV_WIDTH, DEEPNORM_BETA),
        dense(ks[5], D_MODEL, SB_WIDTH),
        dense(ks[6], D_MODEL, SB_WIDTH),
        dense(ks[7], D_MODEL, SB_WIDTH, DEEPNORM_BETA),
        dense(ks[8], D_MODEL, GATE_WIDTH),
    ], axis=-1)
    b_gate = 0.02 * jax.random.normal(ks[9], (DEPTH, GATE_WIDTH), f32)
    sinks = 0.5 * jax.random.normal(ks[10], (DEPTH, SWA_Q_HEADS), f32)
    w_branch_a = dense(ks[11], SWA_Q_WIDTH, D_MODEL, DEEPNORM_BETA)
    w_branch_b = dense(ks[12], SB_WIDTH, D_MODEL, DEEPNORM_BETA)
    w_out = dense(ks[13], D_MODEL, D_MODEL, DEEPNORM_BETA)
    ln1_g = 1.0 + 0.02 * jax.random.normal(ks[14], (DEPTH, D_MODEL), f32)
    ln1_b = 0.02 * jax.random.normal(ks[15], (DEPTH, D_MODEL), f32)
    w_up = dense(ks[16], D_MODEL, 2 * D_FF, DEEPNORM_BETA)
    kc1, kc2 = jax.random.split(ks[17])
    conv_w = jax.random.normal(kc1, (DEPTH, CONV_WIDTH, 2 * D_FF), f32) * (CONV_WIDTH ** -0.5)
    conv_b = 0.02 * jax.random.normal(kc2, (DEPTH, 2 * D_FF), f32)
    w_down = dense(ks[18], D_FF, D_MODEL, DEEPNORM_BETA)
    kl1, kl2 = jax.random.split(ks[19])
    ln2_g = 1.0 + 0.02 * jax.random.normal(kl1, (DEPTH, D_MODEL), f32)
    ln2_b = 0.02 * jax.random.normal(kl2, (DEPTH, D_MODEL), f32)
    return {'x': x, 'positions': positions, 'w_in': w_in, 'b_gate': b_gate, 'sinks': sinks,
            'w_branch_a': w_branch_a, 'w_branch_b': w_branch_b, 'w_out': w_out,
            'ln1_g': ln1_g, 'ln1_b': ln1_b, 'w_up': w_up, 'conv_w': conv_w, 'conv_b': conv_b,
            'w_down': w_down, 'ln2_g': ln2_g, 'ln2_b': ln2_b}


def reference(x, positions, w_in, b_gate, sinks, w_branch_a, w_branch_b, w_out,
              ln1_g, ln1_b, w_up, conv_w, conv_b, w_down, ln2_g, ln2_b):
    B, T, _ = x.shape
    cos, sin = rotary_tables(positions)
    for l in range(DEPTH):
        proj = x @ w_in[l]
        qa, ka, va, qb, kb, vb, gl = jnp.split(proj, IN_SPLITS, axis=-1)
        qa = apply_rope(qa.reshape(B, T, SWA_Q_HEADS, HEAD_DIM), cos, sin)
        ka = apply_rope(ka.reshape(B, T, SWA_KV_HEADS, HEAD_DIM), cos, sin)
        va = va.reshape(B, T, SWA_KV_HEADS, HEAD_DIM)
        ya = sliding_window_sink_attention(qa, ka, va, sinks[l])
        yb = stick_breaking_attention(
            qb.reshape(B, T, SB_HEADS, HEAD_DIM),
            kb.reshape(B, T, SB_HEADS, HEAD_DIM),
            vb.reshape(B, T, SB_HEADS, HEAD_DIM))
        gates = jax.nn.sigmoid(gl + b_gate[l]).reshape(B, T, N_BRANCHES, D_MODEL)
        h = gates[:, :, 0, :] * (ya @ w_branch_a[l]) + gates[:, :, 1, :] * (yb @ w_branch_b[l])
        x = layer_norm(DEEPNORM_ALPHA * x + h @ w_out[l], ln1_g[l], ln1_b[l])
        f = conv_ffn(x, w_up[l], conv_w[l], conv_b[l], w_down[l])
        x = layer_norm(DEEPNORM_ALPHA * x + f, ln2_g[l], ln2_b[l])
    return x
```

```python
import functools
import math

import jax
import jax.numpy as jnp
from jax import lax
from jax.experimental import pallas as pl
from jax.experimental.pallas import tpu as pltpu

D_MODEL = 1024
HEAD_DIM = 64
SWA_Q_HEADS = 8
SWA_KV_HEADS = 2
SB_HEADS = 8
WINDOW = 128
ROPE_THETA = 10000.0
D_FF = 2816
CONV_WIDTH = 3
LN_EPS = 1e-5

SWA_Q_WIDTH = SWA_Q_HEADS * HEAD_DIM
SWA_KV_WIDTH = SWA_KV_HEADS * HEAD_DIM
SB_WIDTH = SB_HEADS * HEAD_DIM
GATE_WIDTH = 2 * D_MODEL
IN_TOTAL = SWA_Q_WIDTH + 2 * SWA_KV_WIDTH + 3 * SB_WIDTH + GATE_WIDTH

LANES = 128
VMEM_LIMIT = 56 * 1024 * 1024

PROJ_TM = 512
MERGE_TM = 512
FFN_TM = 512
FFN_CW = 256
FFN_HALO = 16
SB_TQ = 256
SB_TK = 256

F32 = jnp.float32
BF16 = jnp.bfloat16


def _dot(a, b):
    return jnp.dot(a, b, preferred_element_type=F32)


def _dot_nt(a, b):
    return lax.dot_general(a, b, (((1,), (1,)), ((), ())), preferred_element_type=F32)


def _layer_norm(v, g, b):
    mu = jnp.mean(v, axis=-1, keepdims=True)
    vc = v - mu
    var = jnp.mean(vc * vc, axis=-1, keepdims=True)
    return vc * lax.rsqrt(var + LN_EPS) * g + b


def _proj_kernel(x_ref, pos_ref, invf_ref, sign_ref, w_ref, bg_ref,
                 qa_ref, ka_ref, kas_ref, va_ref, vas_ref, qb_ref, kb_ref, vb_ref, g_ref):
    xb = x_ref[...].astype(BF16)
    ang = pos_ref[...].astype(F32) * invf_ref[...]
    cosv = jnp.cos(ang)
    sinv = jnp.sin(ang) * sign_ref[...]
    lane = lax.broadcasted_iota(jnp.int32, (1, LANES), 1)
    first_half = (lane % HEAD_DIM) < (HEAD_DIM // 2)

    def rope(t):
        fwd = pltpu.roll(t, HEAD_DIM // 2, 1)
        bwd = pltpu.roll(t, LANES - HEAD_DIM // 2, 1)
        rot = jnp.where(first_half, bwd, fwd)
        return t * cosv + rot * sinv

    scale = HEAD_DIM ** -0.5
    off = 0
    t = _dot(xb, w_ref[:, off:off + SWA_Q_WIDTH])
    for c in range(SWA_Q_WIDTH // LANES):
        qa_ref[:, c * LANES:(c + 1) * LANES] = (rope(t[:, c * LANES:(c + 1) * LANES]) * scale).astype(BF16)
    off += SWA_Q_WIDTH
    t = rope(_dot(xb, w_ref[:, off:off + SWA_KV_WIDTH]))
    ka_ref[...] = t.astype(BF16)
    kas_ref[...] = pltpu.roll(t, HEAD_DIM, 1).astype(BF16)
    off += SWA_KV_WIDTH
    t = _dot(xb, w_ref[:, off:off + SWA_KV_WIDTH])
    va_ref[...] = t.astype(BF16)
    vas_ref[...] = pltpu.roll(t, HEAD_DIM, 1).astype(BF16)
    off += SWA_KV_WIDTH
    qb_ref[...] = (_dot(xb, w_ref[:, off:off + SB_WIDTH]) * scale).astype(BF16)
    off += SB_WIDTH
    kb_ref[...] = _dot(xb, w_ref[:, off:off + SB_WIDTH]).astype(BF16)
    off += SB_WIDTH
    vb_ref[...] = _dot(xb, w_ref[:, off:off + SB_WIDTH]).astype(BF16)
    off += SB_WIDTH
    gw = 512
    for c in range(GATE_WIDTH // gw):
        gl = _dot(xb, w_ref[:, off + c * gw:off + (c + 1) * gw]) + bg_ref[:, c * gw:(c + 1) * gw]
        g_ref[:, c * gw:(c + 1) * gw] = jax.nn.sigmoid(gl).astype(BF16)


def _proj_call(x2, pos2, invf, sign, w_in, b_gate):
    m = x2.shape[0]
    tm = PROJ_TM
    row = lambda w: pl.BlockSpec((tm, w), lambda i: (i, 0))
    full = lambda a: pl.BlockSpec(a.shape, lambda i: (0, 0))
    out_widths = [SWA_Q_WIDTH, SWA_KV_WIDTH, SWA_KV_WIDTH, SWA_KV_WIDTH, SWA_KV_WIDTH,
                  SB_WIDTH, SB_WIDTH, SB_WIDTH, GATE_WIDTH]
    return pl.pallas_call(
        _proj_kernel,
        grid=(m // tm,),
        in_specs=[row(D_MODEL), row(1), full(invf), full(sign), full(w_in), full(b_gate)],
        out_specs=[row(w) for w in out_widths],
        out_shape=[jax.ShapeDtypeStruct((m, w), BF16) for w in out_widths],
        compiler_params=pltpu.CompilerParams(
            dimension_semantics=("arbitrary",), vmem_limit_bytes=VMEM_LIMIT),
        name="proj",
    )(x2, pos2, invf, sign, w_in, b_gate)


def _swa_kernel(sinks_ref, q_ref, kp_ref, kc_ref, ksp_ref, ksc_ref,
                vp_ref, vc_ref, vsp_ref, vsc_ref, o_ref):
    i = pl.program_id(1)
    blk = WINDOW
    qi = lax.broadcasted_iota(jnp.int32, (blk, 2 * blk), 0)
    ki = lax.broadcasted_iota(jnp.int32, (blk, 2 * blk), 1)
    rel = qi + blk - ki
    mask = (rel >= 0) & (rel < WINDOW) & ((ki >= blk) | (i > 0))
    lane = lax.broadcasted_iota(jnp.int32, (1, LANES), 1)
    low = lane < HEAD_DIM
    group = SWA_Q_HEADS // SWA_KV_HEADS
    k_plain = jnp.concatenate([kp_ref[...], kc_ref[...]], axis=0)
    k_swap = jnp.concatenate([ksp_ref[...], ksc_ref[...]], axis=0)
    v_plain = jnp.concatenate([vp_ref[...], vc_ref[...]], axis=0)
    v_swap = jnp.concatenate([vsp_ref[...], vsc_ref[...]], axis=0)
    for c in range(SWA_Q_WIDTH // LANES):
        qc = q_ref[:, c * LANES:(c + 1) * LANES]
        outs = []
        for o in range(2):
            h = 2 * c + o
            kv = h // group
            plain = (kv == o)
            kt = k_plain if plain else k_swap
            vt = v_plain if plain else v_swap
            qm = jnp.where(low if o == 0 else ~low, qc, jnp.zeros_like(qc))
            s = _dot_nt(qm, kt)
            s = jnp.where(mask, s, -1e30)
            sink = sinks_ref[h]
            mx = jnp.maximum(jnp.max(s, axis=-1, keepdims=True), sink)
            p = jnp.exp(s - mx)
            denom = jnp.sum(p, axis=-1, keepdims=True) + jnp.exp(sink - mx)
            pv = _dot(p.astype(BF16), vt)
            outs.append(pv / denom)
        o_ref[:, c * LANES:(c + 1) * LANES] = jnp.where(low, outs[0], outs[1]).astype(BF16)


def _swa_call(sinks, qa, ka, kas, va, vas, batch, seq):
    m = qa.shape[0]
    blk = WINDOW
    n = seq // blk
    cur = lambda b, i: (b * n + i, 0)
    prev = lambda b, i: (b * n + jnp.maximum(i - 1, 0), 0)
    kvp = pl.BlockSpec((blk, SWA_KV_WIDTH), prev)
    kvc = pl.BlockSpec((blk, SWA_KV_WIDTH), cur)
    return pl.pallas_call(
        _swa_kernel,
        grid=(batch, n),
        in_specs=[pl.BlockSpec(memory_space=pltpu.SMEM),
                  pl.BlockSpec((blk, SWA_Q_WIDTH), cur),
                  kvp, kvc, kvp, kvc, kvp, kvc, kvp, kvc],
        out_specs=pl.BlockSpec((blk, SWA_Q_WIDTH), cur),
        out_shape=jax.ShapeDtypeStruct((m, SWA_Q_WIDTH), BF16),
        compiler_params=pltpu.CompilerParams(
            dimension_semantics=("arbitrary", "arbitrary"), vmem_limit_bytes=VMEM_LIMIT),
        name="swa",
    )(sinks, qa, ka, ka, kas, kas, va, va, vas, vas)


def _sb_kernel(q_ref, k_ref, v_ref, o_ref):
    i = pl.program_id(2)
    tq, tk = SB_TQ, SB_TK
    lane = lax.broadcasted_iota(jnp.int32, (1, LANES), 1)
    rr = lax.broadcasted_iota(jnp.int32, (tk, tk), 0)
    cc = lax.broadcasted_iota(jnp.int32, (tk, tk), 1)
    incl_tri = jnp.where(rr >= cc, 1.0, 0.0).astype(BF16)
    qrow = lax.broadcasted_iota(jnp.int32, (tq, tk), 0)
    kcol = lax.broadcasted_iota(jnp.int32, (tq, tk), 1)
    causal = kcol < qrow

    def tile(qm, j, carry, acc, diag):
        start = pl.multiple_of(j * tk, tk)
        kt = k_ref[pl.ds(start, tk), :]
        vt = v_ref[pl.ds(start, tk), :]
        z = _dot_nt(qm, kt)
        sp = jnp.maximum(z, 0.0) + jnp.log(1.0 + jnp.exp(-jnp.abs(z)))
        if diag:
            sp = jnp.where(causal, sp, 0.0)
        hi = sp.astype(BF16)
        lo = (sp - hi.astype(F32)).astype(BF16)
        incl = _dot(hi, incl_tri) + _dot(lo, incl_tri)
        a = jnp.exp(z - incl)
        if diag:
            a = jnp.where(causal, a, 0.0)
        pv = _dot(a.astype(BF16), vt)
        acc = acc + jnp.exp(-carry) * pv
        carry = carry + jnp.broadcast_to(incl[:, 0:1], (tq, LANES))
        return carry, acc

    outs = []
    for h in range(2):
        head = (lane // HEAD_DIM) == h
        q = q_ref[...]
        qm = jnp.where(head, q, jnp.zeros_like(q))
        zero = jnp.zeros((tq, LANES), F32)
        carry, acc = tile(qm, i, zero, zero, True)

        def body(n, state, qm=qm):
            return tile(qm, i - 1 - n, state[0], state[1], False)

        carry, acc = lax.fori_loop(0, i, body, (carry, acc))
        outs.append(acc)
    o_ref[...] = jnp.where(lane < HEAD_DIM, outs[0], outs[1]).astype(BF16)


def _sb_call(qb, kb, vb, batch, seq):
    m = qb.shape[0]
    tq = SB_TQ
    nq = seq // tq
    pairs = SB_WIDTH // LANES
    return pl.pallas_call(
        _sb_kernel,
        grid=(batch, pairs, nq),
        in_specs=[pl.BlockSpec((tq, LANES), lambda b, p, i: (b * nq + i, p)),
                  pl.BlockSpec((seq, LANES), lambda b, p, i: (b, p)),
                  pl.BlockSpec((seq, LANES), lambda b, p, i: (b, p))],
        out_specs=pl.BlockSpec((tq, LANES), lambda b, p, i: (b * nq + i, p)),
        out_shape=jax.ShapeDtypeStruct((m, SB_WIDTH), BF16),
        compiler_params=pltpu.CompilerParams(
            dimension_semantics=("arbitrary", "arbitrary", "arbitrary"),
            vmem_limit_bytes=VMEM_LIMIT),
        name="stickbreak",
    )(qb, kb, vb)


def _merge_kernel(alpha, x_ref, ya_ref, yb_ref, g_ref, wa_ref, wb_ref, wo_ref, lg_ref, lb_ref, o_ref):
    pa = _dot(ya_ref[...], wa_ref[...])
    pb = _dot(yb_ref[...], wb_ref[...])
    h = g_ref[:, 0:D_MODEL].astype(F32) * pa + g_ref[:, D_MODEL:2 * D_MODEL].astype(F32) * pb
    r = alpha * x_ref[...] + _dot(h.astype(BF16), wo_ref[...])
    o_ref[...] = _layer_norm(r, lg_ref[...], lb_ref[...])


def _merge_call(alpha, x2, ya, yb, gates, wa, wb, wo, lg, lb):
    m = x2.shape[0]
    tm = MERGE_TM
    row = lambda w: pl.BlockSpec((tm, w), lambda i: (i, 0))
    full = lambda a: pl.BlockSpec(a.shape, lambda i: (0, 0))
    return pl.pallas_call(
        functools.partial(_merge_kernel, alpha),
        grid=(m // tm,),
        in_specs=[row(D_MODEL), row(SWA_Q_WIDTH), row(SB_WIDTH), row(GATE_WIDTH),
                  full(wa), full(wb), full(wo), full(lg), full(lb)],
        out_specs=row(D_MODEL),
        out_shape=jax.ShapeDtypeStruct((m, D_MODEL), F32),
        compiler_params=pltpu.CompilerParams(
            dimension_semantics=("arbitrary",), vmem_limit_bytes=VMEM_LIMIT),
        name="merge",
    )(x2, ya, yb, gates, wa, wb, wo, lg, lb)


def _ffn_kernel(alpha, tiles_per_seq, x_ref, halo_ref, wu_ref, cw_ref, cb_ref, wd_ref,
                lg_ref, lb_ref, o_ref):
    i = pl.program_id(0)
    tm, hl, cwid = FFN_TM, FFN_HALO, FFN_CW
    x = x_ref[...]
    seq_start = (i % tiles_per_seq) == 0
    halo = jnp.where(seq_start, 0.0, halo_ref[...])
    xe = jnp.concatenate([halo.astype(BF16), x.astype(BF16)], axis=0)

    def conv(u, col):
        sl = slice(col, col + cwid)
        y = (cw_ref[2:3, sl] * u[hl:, :]
             + cw_ref[1:2, sl] * pltpu.roll(u, 1, 0)[hl:, :]
             + cw_ref[0:1, sl] * pltpu.roll(u, 2, 0)[hl:, :])
        return y + cb_ref[:, sl]

    acc = jnp.zeros((tm, D_MODEL), F32)
    for c in range(D_FF // cwid):
        gcol = c * cwid
        ucol = D_FF + c * cwid
        gate = conv(_dot(xe, wu_ref[:, gcol:gcol + cwid]), gcol)
        up = conv(_dot(xe, wu_ref[:, ucol:ucol + cwid]), ucol)
        act = (gate * jax.nn.sigmoid(gate) * up).astype(BF16)
        acc = acc + _dot(act, wd_ref[gcol:gcol + cwid, :])
    o_ref[...] = _layer_norm(alpha * x + acc, lg_ref[...], lb_ref[...])


def _ffn_call(alpha, x1, wu, cw, cb, wd, lg, lb, seq):
    m = x1.shape[0]
    tm, hl = FFN_TM, FFN_HALO
    full = lambda a: pl.BlockSpec(a.shape, lambda i: (0, 0))
    halo_blocks = tm // hl
    return pl.pallas_call(
        functools.partial(_ffn_kernel, alpha, seq // tm),
        grid=(m // tm,),
        in_specs=[pl.BlockSpec((tm, D_MODEL), lambda i: (i, 0)),
                  pl.BlockSpec((hl, D_MODEL), lambda i: (jnp.maximum(i * halo_blocks - 1, 0), 0)),
                  full(wu), full(cw), full(cb), full(wd), full(lg), full(lb)],
        out_specs=pl.BlockSpec((tm, D_MODEL), lambda i: (i, 0)),
        out_shape=jax.ShapeDtypeStruct((m, D_MODEL), F32),
        compiler_params=pltpu.CompilerParams(
            dimension_semantics=("arbitrary",), vmem_limit_bytes=VMEM_LIMIT),
        name="convffn",
    )(x1, x1, wu, cw, cb, wd, lg, lb)


def kernel(x, positions, w_in, b_gate, sinks, w_branch_a, w_branch_b, w_out, ln1_g, ln1_b,
           w_up, conv_w, conv_b, w_down, ln2_g, ln2_b):
    batch, seq, _ = x.shape
    depth = w_in.shape[0]
    alpha = (2.0 * depth) ** 0.25
    m = batch * seq
    assert seq % SB_TQ == 0 and seq % FFN_TM == 0 and seq % WINDOW == 0
    assert m % PROJ_TM == 0 and m % MERGE_TM == 0

    half = jnp.arange(0, HEAD_DIM, 2, dtype=F32) / HEAD_DIM
    inv_freq = 1.0 / (ROPE_THETA ** half)
    invf = jnp.tile(inv_freq, LANES // (HEAD_DIM // 2))[None, :]
    sign = jnp.tile(jnp.concatenate([-jnp.ones(HEAD_DIM // 2, F32), jnp.ones(HEAD_DIM // 2, F32)]),
                    LANES // HEAD_DIM)[None, :]
    pos2 = positions.reshape(m, 1)
    x2 = x.reshape(m, D_MODEL)

    for l in range(depth):
        qa, ka, kas, va, vas, qb, kb, vb, gates = _proj_call(
            x2, pos2, invf, sign, w_in[l].astype(BF16), b_gate[l][None, :])
        ya = _swa_call(sinks[l], qa, ka, kas, va, vas, batch, seq)
        yb = _sb_call(qb, kb, vb, batch, seq)
        x1 = _merge_call(alpha, x2, ya, yb, gates,
                         w_branch_a[l].astype(BF16), w_branch_b[l].astype(BF16),
                         w_out[l].astype(BF16), ln1_g[l][None, :], ln1_b[l][None, :])
        x2 = _ffn_call(alpha, x1, w_up[l].astype(BF16), conv_w[l], conv_b[l][None, :],
                       w_down[l].astype(BF16), ln2_g[l][None, :], ln2_b[l][None, :], seq)
    return x2.reshape(batch, seq, D_MODEL)
```

```python
import functools
import math

import jax
import jax.numpy as jnp
from jax import lax
from jax.experimental import pallas as pl
from jax.experimental.pallas import tpu as pltpu

D_MODEL = 1024
HEAD_DIM = 64
SWA_Q_HEADS = 8
SWA_KV_HEADS = 2
SB_HEADS = 8
WINDOW = 128
ROPE_THETA = 10000.0
D_FF = 2816
CONV_WIDTH = 3
LN_EPS = 1e-5

SWA_Q_WIDTH = SWA_Q_HEADS * HEAD_DIM
SWA_KV_WIDTH = SWA_KV_HEADS * HEAD_DIM
SB_WIDTH = SB_HEADS * HEAD_DIM
GATE_WIDTH = 2 * D_MODEL
IN_TOTAL = SWA_Q_WIDTH + 2 * SWA_KV_WIDTH + 3 * SB_WIDTH + GATE_WIDTH

LANES = 128
VMEM_LIMIT = 56 * 1024 * 1024

PROJ_TM = 512
MERGE_TM = 512
FFN_TM = 512
FFN_CW = 256
FFN_HALO = 16
SB_TQ = 256
SB_TK = 256
SB_CHUNKS = 4

LOG2E = math.log2(math.e)

F32 = jnp.float32
BF16 = jnp.bfloat16


def _dot(a, b):
    return jnp.dot(a, b, preferred_element_type=F32)


def _dot_nt(a, b):
    return lax.dot_general(a, b, (((1,), (1,)), ((), ())), preferred_element_type=F32)


def _layer_norm(v, g, b):
    mu = jnp.mean(v, axis=-1, keepdims=True)
    vc = v - mu
    var = jnp.mean(vc * vc, axis=-1, keepdims=True)
    return vc * lax.rsqrt(var + LN_EPS) * g + b


def _proj_kernel(x_ref, pos_ref, invf_ref, sign_ref, w_ref, bg_ref,
                 qa_ref, ka_ref, kas_ref, va_ref, vas_ref, qb_ref, kb_ref, vb_ref, g_ref):
    xb = x_ref[...].astype(BF16)
    ang = pos_ref[...].astype(F32) * invf_ref[...]
    cosv = jnp.cos(ang)
    sinv = jnp.sin(ang) * sign_ref[...]
    lane = lax.broadcasted_iota(jnp.int32, (1, LANES), 1)
    first_half = (lane % HEAD_DIM) < (HEAD_DIM // 2)

    def rope(t):
        fwd = pltpu.roll(t, HEAD_DIM // 2, 1)
        bwd = pltpu.roll(t, LANES - HEAD_DIM // 2, 1)
        rot = jnp.where(first_half, bwd, fwd)
        return t * cosv + rot * sinv

    scale = HEAD_DIM ** -0.5
    off = 0
    t = _dot(xb, w_ref[:, off:off + SWA_Q_WIDTH])
    for c in range(SWA_Q_WIDTH // LANES):
        qa_ref[:, c * LANES:(c + 1) * LANES] = (rope(t[:, c * LANES:(c + 1) * LANES]) * scale).astype(BF16)
    off += SWA_Q_WIDTH
    t = rope(_dot(xb, w_ref[:, off:off + SWA_KV_WIDTH]))
    ka_ref[...] = t.astype(BF16)
    kas_ref[...] = pltpu.roll(t, HEAD_DIM, 1).astype(BF16)
    off += SWA_KV_WIDTH
    t = _dot(xb, w_ref[:, off:off + SWA_KV_WIDTH])
    va_ref[...] = t.astype(BF16)
    vas_ref[...] = pltpu.roll(t, HEAD_DIM, 1).astype(BF16)
    off += SWA_KV_WIDTH
    qb_ref[...] = (_dot(xb, w_ref[:, off:off + SB_WIDTH]) * (scale * LOG2E)).astype(BF16)
    off += SB_WIDTH
    kb_ref[...] = _dot(xb, w_ref[:, off:off + SB_WIDTH]).astype(BF16)
    off += SB_WIDTH
    vb_ref[...] = _dot(xb, w_ref[:, off:off + SB_WIDTH]).astype(BF16)
    off += SB_WIDTH
    gw = 512
    for c in range(GATE_WIDTH // gw):
        gl = _dot(xb, w_ref[:, off + c * gw:off + (c + 1) * gw]) + bg_ref[:, c * gw:(c + 1) * gw]
        g_ref[:, c * gw:(c + 1) * gw] = jax.nn.sigmoid(gl).astype(BF16)


def _proj_call(x2, pos2, invf, sign, w_in, b_gate):
    m = x2.shape[0]
    tm = PROJ_TM
    row = lambda w: pl.BlockSpec((tm, w), lambda i: (i, 0))
    full = lambda a: pl.BlockSpec(a.shape, lambda i: (0, 0))
    out_widths = [SWA_Q_WIDTH, SWA_KV_WIDTH, SWA_KV_WIDTH, SWA_KV_WIDTH, SWA_KV_WIDTH,
                  SB_WIDTH, SB_WIDTH, SB_WIDTH, GATE_WIDTH]
    return pl.pallas_call(
        _proj_kernel,
        grid=(m // tm,),
        in_specs=[row(D_MODEL), row(1), full(invf), full(sign), full(w_in), full(b_gate)],
        out_specs=[row(w) for w in out_widths],
        out_shape=[jax.ShapeDtypeStruct((m, w), BF16) for w in out_widths],
        compiler_params=pltpu.CompilerParams(
            dimension_semantics=("arbitrary",), vmem_limit_bytes=VMEM_LIMIT),
        name="proj",
    )(x2, pos2, invf, sign, w_in, b_gate)


def _swa_kernel(sinks_ref, q_ref, kp_ref, kc_ref, ksp_ref, ksc_ref,
                vp_ref, vc_ref, vsp_ref, vsc_ref, o_ref):
    i = pl.program_id(1)
    blk = WINDOW
    qi = lax.broadcasted_iota(jnp.int32, (blk, 2 * blk), 0)
    ki = lax.broadcasted_iota(jnp.int32, (blk, 2 * blk), 1)
    rel = qi + blk - ki
    mask = (rel >= 0) & (rel < WINDOW) & ((ki >= blk) | (i > 0))
    lane = lax.broadcasted_iota(jnp.int32, (1, LANES), 1)
    low = lane < HEAD_DIM
    group = SWA_Q_HEADS // SWA_KV_HEADS
    k_plain = jnp.concatenate([kp_ref[...], kc_ref[...]], axis=0)
    k_swap = jnp.concatenate([ksp_ref[...], ksc_ref[...]], axis=0)
    v_plain = jnp.concatenate([vp_ref[...], vc_ref[...]], axis=0)
    v_swap = jnp.concatenate([vsp_ref[...], vsc_ref[...]], axis=0)
    for c in range(SWA_Q_WIDTH // LANES):
        qc = q_ref[:, c * LANES:(c + 1) * LANES]
        outs = []
        for o in range(2):
            h = 2 * c + o
            kv = h // group
            plain = (kv == o)
            kt = k_plain if plain else k_swap
            vt = v_plain if plain else v_swap
            qm = jnp.where(low if o == 0 else ~low, qc, jnp.zeros_like(qc))
            s = _dot_nt(qm, kt)
            s = jnp.where(mask, s, -1e30)
            sink = sinks_ref[h]
            mx = jnp.maximum(jnp.max(s, axis=-1, keepdims=True), sink)
            p = jnp.exp(s - mx)
            denom = jnp.sum(p, axis=-1, keepdims=True) + jnp.exp(sink - mx)
            pv = _dot(p.astype(BF16), vt)
            outs.append(pv / denom)
        o_ref[:, c * LANES:(c + 1) * LANES] = jnp.where(low, outs[0], outs[1]).astype(BF16)


def _swa_call(sinks, qa, ka, kas, va, vas, batch, seq):
    m = qa.shape[0]
    blk = WINDOW
    n = seq // blk
    cur = lambda b, i: (b * n + i, 0)
    prev = lambda b, i: (b * n + jnp.maximum(i - 1, 0), 0)
    kvp = pl.BlockSpec((blk, SWA_KV_WIDTH), prev)
    kvc = pl.BlockSpec((blk, SWA_KV_WIDTH), cur)
    return pl.pallas_call(
        _swa_kernel,
        grid=(batch, n),
        in_specs=[pl.BlockSpec(memory_space=pltpu.SMEM),
                  pl.BlockSpec((blk, SWA_Q_WIDTH), cur),
                  kvp, kvc, kvp, kvc, kvp, kvc, kvp, kvc],
        out_specs=pl.BlockSpec((blk, SWA_Q_WIDTH), cur),
        out_shape=jax.ShapeDtypeStruct((m, SWA_Q_WIDTH), BF16),
        compiler_params=pltpu.CompilerParams(
            dimension_semantics=("arbitrary", "arbitrary"), vmem_limit_bytes=VMEM_LIMIT),
        name="swa",
    )(sinks, qa, ka, ka, kas, kas, va, va, vas, vas)


def _softplus2(z):
    neg_abs = lax.bitcast_convert_type(
        lax.bitcast_convert_type(z, jnp.uint32) | jnp.uint32(0x80000000), F32)
    return jnp.maximum(z, 0.0) + jnp.log2(1.0 + jnp.exp2(neg_abs))


def _sb_kernel(q_ref, k_ref, v_ref, o_ref, acc_ref, carry_ref):
    i = pl.program_id(2)
    tq, tk = SB_TQ, SB_TK
    chunks = q_ref.shape[1] // LANES
    lane = lax.broadcasted_iota(jnp.int32, (1, LANES), 1)
    low = lane < HEAD_DIM
    rr = lax.broadcasted_iota(jnp.int32, (tk, tk), 0)
    cc = lax.broadcasted_iota(jnp.int32, (tk, tk), 1)
    incl_tri = jnp.where(rr >= cc, 1.0, 0.0).astype(BF16)
    qms = []
    for c in range(chunks):
        q = q_ref[:, c * LANES:(c + 1) * LANES]
        qms.append(jnp.where(low, q, jnp.zeros_like(q)))
        qms.append(jnp.where(low, jnp.zeros_like(q), q))
    heads = range(2 * chunks)

    def tile(j, diag):
        start = pl.multiple_of(j * tk, tk)
        kts = [k_ref[pl.ds(start, tk), c * LANES:(c + 1) * LANES] for c in range(chunks)]
        vts = [v_ref[pl.ds(start, tk), c * LANES:(c + 1) * LANES] for c in range(chunks)]
        if diag:
            causal = (lax.broadcasted_iota(jnp.int32, (tq, tk), 1)
                      < lax.broadcasted_iota(jnp.int32, (tq, tk), 0))
        zs = [_dot_nt(qms[h], kts[h // 2]) for h in heads]
        sps = [_softplus2(z) for z in zs]
        if diag:
            sps = [jnp.where(causal, sp, 0.0) for sp in sps]
        his = [sp.astype(BF16) for sp in sps]
        los = [(sp - hi.astype(F32)).astype(BF16) for sp, hi in zip(sps, his)]
        incls = [_dot(hi, incl_tri) + _dot(lo, incl_tri) for hi, lo in zip(his, los)]
        aa = [jnp.exp2(z - incl) for z, incl in zip(zs, incls)]
        if diag:
            aa = [jnp.where(causal, a, 0.0) for a in aa]
        pvs = [_dot(aa[h].astype(BF16), vts[h // 2]) for h in heads]
        sums = [jnp.broadcast_to(incl[:, 0:1], (tq, LANES)) for incl in incls]
        pv = [jnp.where(low, pvs[2 * c], pvs[2 * c + 1]) for c in range(chunks)]
        mass = [jnp.where(low, sums[2 * c], sums[2 * c + 1]) for c in range(chunks)]
        return pv, mass

    pv, mass = tile(i, True)
    for c in range(chunks):
        acc_ref[:, c * LANES:(c + 1) * LANES] = pv[c]
        carry_ref[:, c * LANES:(c + 1) * LANES] = mass[c]

    @pl.loop(0, i)
    def _(n):
        pv, mass = tile(i - 1 - n, False)
        for c in range(chunks):
            sl = slice(c * LANES, (c + 1) * LANES)
            carry = carry_ref[:, sl]
            acc_ref[:, sl] += jnp.exp2(-carry) * pv[c]
            carry_ref[:, sl] = carry + mass[c]

    o_ref[...] = acc_ref[...].astype(BF16)


def _sb_call(qb, kb, vb, batch, seq):
    m = qb.shape[0]
    tq = SB_TQ
    nq = seq // tq
    width = SB_CHUNKS * LANES
    groups = SB_WIDTH // width
    return pl.pallas_call(
        _sb_kernel,
        grid=(batch, groups, nq),
        in_specs=[pl.BlockSpec((tq, width), lambda b, p, i: (b * nq + i, p)),
                  pl.BlockSpec((seq, width), lambda b, p, i: (b, p)),
                  pl.BlockSpec((seq, width), lambda b, p, i: (b, p))],
        out_specs=pl.BlockSpec((tq, width), lambda b, p, i: (b * nq + i, p)),
        out_shape=jax.ShapeDtypeStruct((m, SB_WIDTH), BF16),
        scratch_shapes=[pltpu.VMEM((tq, width), F32), pltpu.VMEM((tq, width), F32)],
        compiler_params=pltpu.CompilerParams(
            dimension_semantics=("arbitrary", "arbitrary", "arbitrary"),
            vmem_limit_bytes=VMEM_LIMIT),
        name="stickbreak",
    )(qb, kb, vb)


def _merge_kernel(alpha, x_ref, ya_ref, yb_ref, g_ref, wa_ref, wb_ref, wo_ref, lg_ref, lb_ref, o_ref):
    pa = _dot(ya_ref[...], wa_ref[...])
    pb = _dot(yb_ref[...], wb_ref[...])
    h = g_ref[:, 0:D_MODEL].astype(F32) * pa + g_ref[:, D_MODEL:2 * D_MODEL].astype(F32) * pb
    r = alpha * x_ref[...] + _dot(h.astype(BF16), wo_ref[...])
    o_ref[...] = _layer_norm(r, lg_ref[...], lb_ref[...])


def _merge_call(alpha, x2, ya, yb, gates, wa, wb, wo, lg, lb):
    m = x2.shape[0]
    tm = MERGE_TM
    row = lambda w: pl.BlockSpec((tm, w), lambda i: (i, 0))
    full = lambda a: pl.BlockSpec(a.shape, lambda i: (0, 0))
    return pl.pallas_call(
        functools.partial(_merge_kernel, alpha),
        grid=(m // tm,),
        in_specs=[row(D_MODEL), row(SWA_Q_WIDTH), row(SB_WIDTH), row(GATE_WIDTH),
                  full(wa), full(wb), full(wo), full(lg), full(lb)],
        out_specs=row(D_MODEL),
        out_shape=jax.ShapeDtypeStruct((m, D_MODEL), F32),
        compiler_params=pltpu.CompilerParams(
            dimension_semantics=("arbitrary",), vmem_limit_bytes=VMEM_LIMIT),
        name="merge",
    )(x2, ya, yb, gates, wa, wb, wo, lg, lb)


def _ffn_kernel(alpha, tiles_per_seq, x_ref, halo_ref, wu_ref, cw_ref, cb_ref, wd_ref,
                lg_ref, lb_ref, o_ref):
    i = pl.program_id(0)
    tm, hl, cwid = FFN_TM, FFN_HALO, FFN_CW
    x = x_ref[...]
    seq_start = (i % tiles_per_seq) == 0
    halo = jnp.where(seq_start, 0.0, halo_ref[...])
    xe = jnp.concatenate([halo.astype(BF16), x.astype(BF16)], axis=0)

    def conv(u, col):
        sl = slice(col, col + cwid)
        y = (cw_ref[2:3, sl] * u[hl:, :]
             + cw_ref[1:2, sl] * pltpu.roll(u, 1, 0)[hl:, :]
             + cw_ref[0:1, sl] * pltpu.roll(u, 2, 0)[hl:, :])
        return y + cb_ref[:, sl]

    acc = jnp.zeros((tm, D_MODEL), F32)
    for c in range(D_FF // cwid):
        gcol = c * cwid
        ucol = D_FF + c * cwid
        gate = conv(_dot(xe, wu_ref[:, gcol:gcol + cwid]), gcol)
        up = conv(_dot(xe, wu_ref[:, ucol:ucol + cwid]), ucol)
        act = (gate * jax.nn.sigmoid(gate) * up).astype(BF16)
        acc = acc + _dot(act, wd_ref[gcol:gcol + cwid, :])
    o_ref[...] = _layer_norm(alpha * x + acc, lg_ref[...], lb_ref[...])


def _ffn_call(alpha, x1, wu, cw, cb, wd, lg, lb, seq):
    m = x1.shape[0]
    tm, hl = FFN_TM, FFN_HALO
    full = lambda a: pl.BlockSpec(a.shape, lambda i: (0, 0))
    halo_blocks = tm // hl
    return pl.pallas_call(
        functools.partial(_ffn_kernel, alpha, seq // tm),
        grid=(m // tm,),
        in_specs=[pl.BlockSpec((tm, D_MODEL), lambda i: (i, 0)),
                  pl.BlockSpec((hl, D_MODEL), lambda i: (jnp.maximum(i * halo_blocks - 1, 0), 0)),
                  full(wu), full(cw), full(cb), full(wd), full(lg), full(lb)],
        out_specs=pl.BlockSpec((tm, D_MODEL), lambda i: (i, 0)),
        out_shape=jax.ShapeDtypeStruct((m, D_MODEL), F32),
        compiler_params=pltpu.CompilerParams(
            dimension_semantics=("arbitrary",), vmem_limit_bytes=VMEM_LIMIT),
        name="convffn",
    )(x1, x1, wu, cw, cb, wd, lg, lb)


def kernel(x, positions, w_in, b_gate, sinks, w_branch_a, w_branch_b, w_out, ln1_g, ln1_b,
           w_up, conv_w, conv_b, w_down, ln2_g, ln2_b):
    batch, seq, _ = x.shape
    depth = w_in.shape[0]
    alpha = (2.0 * depth) ** 0.25
    m = batch * seq
    assert seq % SB_TQ == 0 and seq % FFN_TM == 0 and seq % WINDOW == 0
    assert m % PROJ_TM == 0 and m % MERGE_TM == 0

    half = jnp.arange(0, HEAD_DIM, 2, dtype=F32) / HEAD_DIM
    inv_freq = 1.0 / (ROPE_THETA ** half)
    invf = jnp.tile(inv_freq, LANES // (HEAD_DIM // 2))[None, :]
    sign = jnp.tile(jnp.concatenate([-jnp.ones(HEAD_DIM // 2, F32), jnp.ones(HEAD_DIM // 2, F32)]),
                    LANES // HEAD_DIM)[None, :]
    pos2 = positions.reshape(m, 1)
    x2 = x.reshape(m, D_MODEL)

    for l in range(depth):
        qa, ka, kas, va, vas, qb, kb, vb, gates = _proj_call(
            x2, pos2, invf, sign, w_in[l].astype(BF16), b_gate[l][None, :])
        ya = _swa_call(sinks[l], qa, ka, kas, va, vas, batch, seq)
        yb = _sb_call(qb, kb, vb, batch, seq)
        x1 = _merge_call(alpha, x2, ya, yb, gates,
                         w_branch_a[l].astype(BF16), w_branch_b[l].astype(BF16),
                         w_out[l].astype(BF16), ln1_g[l][None, :], ln1_b[l][None, :])
        x2 = _ffn_call(alpha, x1, w_up[l].astype(BF16), conv_w[l], conv_b[l][None, :],
                       w_down[l].astype(BF16), ln2_g[l][None, :], ln2_b[l][None, :], seq)
    return x2.reshape(batch, seq, D_MODEL)
```

```python
import functools
import math

import jax
import jax.numpy as jnp
from jax import lax
from jax.experimental import pallas as pl
from jax.experimental.pallas import tpu as pltpu

D_MODEL = 1024
HEAD_DIM = 64
SWA_Q_HEADS = 8
SWA_KV_HEADS = 2
SB_HEADS = 8
WINDOW = 128
ROPE_THETA = 10000.0
D_FF = 2816
CONV_WIDTH = 3
LN_EPS = 1e-5

SWA_Q_WIDTH = SWA_Q_HEADS * HEAD_DIM
SWA_KV_WIDTH = SWA_KV_HEADS * HEAD_DIM
SB_WIDTH = SB_HEADS * HEAD_DIM
GATE_WIDTH = 2 * D_MODEL
IN_TOTAL = SWA_Q_WIDTH + 2 * SWA_KV_WIDTH + 3 * SB_WIDTH + GATE_WIDTH

LANES = 128
VMEM_LIMIT = 56 * 1024 * 1024

PROJ_TM = 512
MERGE_TM = 512
MERGE_SUB = 128
FFN_TM = 512
FFN_CW = 256
FFN_AHEAD = 11
FFN_HALO = 16
SB_TQ = 256
SB_TK = 256
SB_CHUNKS = 4

LOG2E = math.log2(math.e)

F32 = jnp.float32
BF16 = jnp.bfloat16


def _dot(a, b):
    return jnp.dot(a, b, preferred_element_type=F32)


def _dot_nt(a, b):
    return lax.dot_general(a, b, (((1,), (1,)), ((), ())), preferred_element_type=F32)


def _resident_spec(a):
    return pl.BlockSpec(a.shape, lambda i: (0, 0), pipeline_mode=pl.Buffered(1))


def _layer_norm(v, g, b):
    mu = jnp.mean(v, axis=-1, keepdims=True)
    vc = v - mu
    var = jnp.mean(vc * vc, axis=-1, keepdims=True)
    return vc * lax.rsqrt(var + LN_EPS) * g + b


def _proj_kernel(x_ref, pos_ref, invf_ref, sign_ref, w_ref, bg_ref,
                 qa_ref, ka_ref, kas_ref, va_ref, vas_ref, qb_ref, kb_ref, vb_ref, g_ref):
    xb = x_ref[...].astype(BF16)
    ang = pos_ref[...].astype(F32) * invf_ref[...]
    cosv = jnp.cos(ang)
    sinv = jnp.sin(ang) * sign_ref[...]
    lane = lax.broadcasted_iota(jnp.int32, (1, LANES), 1)
    first_half = (lane % HEAD_DIM) < (HEAD_DIM // 2)

    def rope(t):
        fwd = pltpu.roll(t, HEAD_DIM // 2, 1)
        bwd = pltpu.roll(t, LANES - HEAD_DIM // 2, 1)
        rot = jnp.where(first_half, bwd, fwd)
        return t * cosv + rot * sinv

    scale = HEAD_DIM ** -0.5
    qa_off = 0
    ka_off = qa_off + SWA_Q_WIDTH
    va_off = ka_off + SWA_KV_WIDTH
    qb_off = va_off + SWA_KV_WIDTH
    kb_off = qb_off + SB_WIDTH
    vb_off = kb_off + SB_WIDTH
    g_off = vb_off + SB_WIDTH
    t = _dot(xb, w_ref[:, qa_off:qa_off + SWA_Q_WIDTH])
    for c in range(SWA_Q_WIDTH // LANES):
        qa_ref[:, c * LANES:(c + 1) * LANES] = (rope(t[:, c * LANES:(c + 1) * LANES]) * scale).astype(BF16)
    t = rope(_dot(xb, w_ref[:, ka_off:ka_off + SWA_KV_WIDTH]))
    ka_ref[...] = t.astype(BF16)
    kas_ref[...] = pltpu.roll(t, HEAD_DIM, 1).astype(BF16)
    t = _dot(xb, w_ref[:, va_off:va_off + SWA_KV_WIDTH])
    va_ref[...] = t.astype(BF16)
    vas_ref[...] = pltpu.roll(t, HEAD_DIM, 1).astype(BF16)
    qb_ref[...] = (_dot(xb, w_ref[:, qb_off:qb_off + SB_WIDTH]) * (scale * LOG2E)).astype(BF16)
    kb_ref[...] = _dot(xb, w_ref[:, kb_off:kb_off + SB_WIDTH]).astype(BF16)
    vb_ref[...] = _dot(xb, w_ref[:, vb_off:vb_off + SB_WIDTH]).astype(BF16)
    gw = 512
    for c in range(GATE_WIDTH // gw):
        gl = (_dot(xb, w_ref[:, g_off + c * gw:g_off + (c + 1) * gw])
              + bg_ref[:, c * gw:(c + 1) * gw])
        g_ref[:, c * gw:(c + 1) * gw] = jax.nn.sigmoid(gl).astype(BF16)


def _proj_call(x2, pos2, invf, sign, w_in, b_gate):
    m = x2.shape[0]
    tm = PROJ_TM
    row = lambda w: pl.BlockSpec((tm, w), lambda i: (i, 0))
    full = _resident_spec
    out_widths = [SWA_Q_WIDTH, SWA_KV_WIDTH, SWA_KV_WIDTH, SWA_KV_WIDTH, SWA_KV_WIDTH,
                  SB_WIDTH, SB_WIDTH, SB_WIDTH, GATE_WIDTH]
    return pl.pallas_call(
        _proj_kernel,
        grid=(m // tm,),
        in_specs=[row(D_MODEL), row(1), full(invf), full(sign), full(w_in), full(b_gate)],
        out_specs=[row(w) for w in out_widths],
        out_shape=[jax.ShapeDtypeStruct((m, w), BF16) for w in out_widths],
        compiler_params=pltpu.CompilerParams(
            dimension_semantics=("arbitrary",), vmem_limit_bytes=VMEM_LIMIT),
        name="proj",
    )(x2, pos2, invf, sign, w_in, b_gate)


def _swa_kernel(sinks_ref, q_ref, kp_ref, kc_ref, ksp_ref, ksc_ref,
                vp_ref, vc_ref, vsp_ref, vsc_ref, o_ref):
    i = pl.program_id(1)
    blk = WINDOW
    qi = lax.broadcasted_iota(jnp.int32, (blk, 2 * blk), 0)
    ki = lax.broadcasted_iota(jnp.int32, (blk, 2 * blk), 1)
    rel = qi + blk - ki
    mask = (rel >= 0) & (rel < WINDOW) & ((ki >= blk) | (i > 0))
    lane = lax.broadcasted_iota(jnp.int32, (1, LANES), 1)
    low = lane < HEAD_DIM
    group = SWA_Q_HEADS // SWA_KV_HEADS
    k_plain = jnp.concatenate([kp_ref[...], kc_ref[...]], axis=0)
    k_swap = jnp.concatenate([ksp_ref[...], ksc_ref[...]], axis=0)
    v_plain = jnp.concatenate([vp_ref[...], vc_ref[...]], axis=0)
    v_swap = jnp.concatenate([vsp_ref[...], vsc_ref[...]], axis=0)
    heads = range(SWA_Q_HEADS)
    plain = [(h // group) == (h % 2) for h in heads]
    qms = []
    for h in heads:
        qc = q_ref[:, (h // 2) * LANES:(h // 2 + 1) * LANES]
        qms.append(jnp.where(low if h % 2 == 0 else ~low, qc, jnp.zeros_like(qc)))
    ss = [jnp.where(mask, _dot_nt(qms[h], k_plain if plain[h] else k_swap), -1e30)
          for h in heads]
    mxs = [jnp.maximum(jnp.max(ss[h], axis=-1, keepdims=True), sinks_ref[h]) for h in heads]
    ps = [jnp.exp(ss[h] - mxs[h]) for h in heads]
    denoms = [jnp.sum(ps[h], axis=-1, keepdims=True) + jnp.exp(sinks_ref[h] - mxs[h])
              for h in heads]
    pvs = [_dot(ps[h].astype(BF16), v_plain if plain[h] else v_swap) for h in heads]
    outs = [pvs[h] / denoms[h] for h in heads]
    for c in range(SWA_Q_WIDTH // LANES):
        o_ref[:, c * LANES:(c + 1) * LANES] = jnp.where(
            low, outs[2 * c], outs[2 * c + 1]).astype(BF16)


def _swa_call(sinks, qa, ka, kas, va, vas, batch, seq):
    m = qa.shape[0]
    blk = WINDOW
    n = seq // blk
    cur = lambda b, i: (b * n + i, 0)
    prev = lambda b, i: (b * n + jnp.maximum(i - 1, 0), 0)
    kvp = pl.BlockSpec((blk, SWA_KV_WIDTH), prev)
    kvc = pl.BlockSpec((blk, SWA_KV_WIDTH), cur)
    return pl.pallas_call(
        _swa_kernel,
        grid=(batch, n),
        in_specs=[pl.BlockSpec(memory_space=pltpu.SMEM),
                  pl.BlockSpec((blk, SWA_Q_WIDTH), cur),
                  kvp, kvc, kvp, kvc, kvp, kvc, kvp, kvc],
        out_specs=pl.BlockSpec((blk, SWA_Q_WIDTH), cur),
        out_shape=jax.ShapeDtypeStruct((m, SWA_Q_WIDTH), BF16),
        compiler_params=pltpu.CompilerParams(
            dimension_semantics=("arbitrary", "arbitrary"), vmem_limit_bytes=VMEM_LIMIT),
        name="swa",
    )(sinks, qa, ka, ka, kas, kas, va, va, vas, vas)


def _softplus2(z):
    neg_abs = lax.bitcast_convert_type(
        lax.bitcast_convert_type(z, jnp.uint32) | jnp.uint32(0x80000000), F32)
    return jnp.maximum(z, 0.0) + jnp.log2(1.0 + jnp.exp2(neg_abs))


def _sb_kernel(q_ref, k_ref, v_ref, o_ref, acc_ref, carry_ref, factor_ref):
    i = pl.program_id(2)
    tq, tk = SB_TQ, SB_TK
    chunks = q_ref.shape[1] // LANES
    lane = lax.broadcasted_iota(jnp.int32, (1, LANES), 1)
    low = lane < HEAD_DIM
    rr = lax.broadcasted_iota(jnp.int32, (tk, tk), 0)
    cc = lax.broadcasted_iota(jnp.int32, (tk, tk), 1)
    incl_tri = jnp.where(rr >= cc, 1.0, 0.0).astype(BF16)
    qms = []
    for c in range(chunks):
        q = q_ref[:, c * LANES:(c + 1) * LANES]
        qms.append(jnp.where(low, q, jnp.zeros_like(q)))
        qms.append(jnp.where(low, jnp.zeros_like(q), q))
    heads = range(2 * chunks)

    def tile(j, diag):
        start = pl.multiple_of(j * tk, tk)
        kts = [k_ref[pl.ds(start, tk), c * LANES:(c + 1) * LANES] for c in range(chunks)]
        vts = [v_ref[pl.ds(start, tk), c * LANES:(c + 1) * LANES] for c in range(chunks)]
        if diag:
            causal = (lax.broadcasted_iota(jnp.int32, (tq, tk), 1)
                      < lax.broadcasted_iota(jnp.int32, (tq, tk), 0))
        zs = [_dot_nt(qms[h], kts[h // 2]) for h in heads]
        sps = [_softplus2(z) for z in zs]
        if diag:
            sps = [jnp.where(causal, sp, 0.0) for sp in sps]
        his = [sp.astype(BF16) for sp in sps]
        los = [(sp - hi.astype(F32)).astype(BF16) for sp, hi in zip(sps, his)]
        incls = [_dot(hi, incl_tri) + _dot(lo, incl_tri) for hi, lo in zip(his, los)]
        aa = [jnp.exp2(z - incl) for z, incl in zip(zs, incls)]
        if diag:
            aa = [jnp.where(causal, a, 0.0) for a in aa]
        pvs = [_dot(aa[h].astype(BF16), vts[h // 2]) for h in heads]
        sums = [jnp.broadcast_to(incl[:, 0:1], (tq, LANES)) for incl in incls]
        pv = [jnp.where(low, pvs[2 * c], pvs[2 * c + 1]) for c in range(chunks)]
        mass = [jnp.where(low, sums[2 * c], sums[2 * c + 1]) for c in range(chunks)]
        return pv, mass

    def advance(mass, first):
        fmax = None
        for c in range(chunks):
            sl = slice(c * LANES, (c + 1) * LANES)
            carry = mass[c] if first else carry_ref[:, sl] + mass[c]
            carry_ref[:, sl] = carry
            f = jnp.exp2(-carry)
            factor_ref[:, sl] = f
            fmax = f if fmax is None else jnp.maximum(fmax, f)
        return (jnp.max(fmax) > 0.0).astype(jnp.int32)

    pv, mass = tile(i, True)
    for c in range(chunks):
        acc_ref[:, c * LANES:(c + 1) * LANES] = pv[c]
    alive = advance(mass, True)

    def cond(state):
        n, alive = state
        return (n < i) & (alive > 0)

    def body(state):
        n, _ = state
        pv, mass = tile(i - 1 - n, False)
        for c in range(chunks):
            sl = slice(c * LANES, (c + 1) * LANES)
            acc_ref[:, sl] += factor_ref[:, sl] * pv[c]
        return n + 1, advance(mass, False)

    lax.while_loop(cond, body, (jnp.int32(0), alive))
    o_ref[...] = acc_ref[...].astype(BF16)


def _sb_call(qb, kb, vb, batch, seq):
    m = qb.shape[0]
    tq = SB_TQ
    nq = seq // tq
    width = SB_CHUNKS * LANES
    groups = SB_WIDTH // width
    return pl.pallas_call(
        _sb_kernel,
        grid=(batch, groups, nq),
        in_specs=[pl.BlockSpec((tq, width), lambda b, p, i: (b * nq + i, p)),
                  pl.BlockSpec((seq, width), lambda b, p, i: (b, p)),
                  pl.BlockSpec((seq, width), lambda b, p, i: (b, p))],
        out_specs=pl.BlockSpec((tq, width), lambda b, p, i: (b * nq + i, p)),
        out_shape=jax.ShapeDtypeStruct((m, SB_WIDTH), BF16),
        scratch_shapes=[pltpu.VMEM((tq, width), F32)] * 3,
        compiler_params=pltpu.CompilerParams(
            dimension_semantics=("arbitrary", "arbitrary", "arbitrary"),
            vmem_limit_bytes=VMEM_LIMIT),
        name="stickbreak",
    )(qb, kb, vb)


def _merge_kernel(alpha, x_ref, ya_ref, yb_ref, g_ref, wa_ref, wb_ref, wo_ref, lg_ref, lb_ref, o_ref):
    sub = MERGE_SUB
    rows = [slice(r, r + sub) for r in range(0, MERGE_TM, sub)]
    pas = [_dot(ya_ref[r, :], wa_ref[...]) for r in rows]
    pbs = [_dot(yb_ref[r, :], wb_ref[...]) for r in rows]
    hs = [(g_ref[r, 0:D_MODEL].astype(F32) * pa
           + g_ref[r, D_MODEL:2 * D_MODEL].astype(F32) * pb).astype(BF16)
          for r, pa, pb in zip(rows, pas, pbs)]
    res = [alpha * x_ref[r, :] + _dot(h, wo_ref[...]) for r, h in zip(rows, hs)]
    for r, v in zip(rows, res):
        o_ref[r, :] = _layer_norm(v, lg_ref[...], lb_ref[...])


def _merge_call(alpha, x2, ya, yb, gates, wa, wb, wo, lg, lb):
    m = x2.shape[0]
    tm = MERGE_TM
    row = lambda w: pl.BlockSpec((tm, w), lambda i: (i, 0))
    full = _resident_spec
    return pl.pallas_call(
        functools.partial(_merge_kernel, alpha),
        grid=(m // tm,),
        in_specs=[row(D_MODEL), row(SWA_Q_WIDTH), row(SB_WIDTH), row(GATE_WIDTH),
                  full(wa), full(wb), full(wo), full(lg), full(lb)],
        out_specs=row(D_MODEL),
        out_shape=jax.ShapeDtypeStruct((m, D_MODEL), F32),
        compiler_params=pltpu.CompilerParams(
            dimension_semantics=("arbitrary",), vmem_limit_bytes=VMEM_LIMIT),
        name="merge",
    )(x2, ya, yb, gates, wa, wb, wo, lg, lb)


def _ffn_kernel(alpha, tiles_per_seq, x_ref, halo_ref, wu_ref, cw_ref, cb_ref, wd_ref,
                lg_ref, lb_ref, o_ref):
    i = pl.program_id(0)
    tm, hl, cwid = FFN_TM, FFN_HALO, FFN_CW
    x = x_ref[...]
    seq_start = (i % tiles_per_seq) == 0
    halo = jnp.where(seq_start, 0.0, halo_ref[...])
    xe = jnp.concatenate([halo.astype(BF16), x.astype(BF16)], axis=0)

    def conv(u, col):
        sl = slice(col, col + cwid)
        y = (cw_ref[2:3, sl] * u[hl:, :]
             + cw_ref[1:2, sl] * pltpu.roll(u, 1, 0)[hl:, :]
             + cw_ref[0:1, sl] * pltpu.roll(u, 2, 0)[hl:, :])
        return y + cb_ref[:, sl]

    def up_dots(c):
        gcol = c * cwid
        ucol = D_FF + c * cwid
        return _dot(xe, wu_ref[:, gcol:gcol + cwid]), _dot(xe, wu_ref[:, ucol:ucol + cwid])

    nchunks = D_FF // cwid
    acc = jnp.zeros((tm, D_MODEL), F32)
    ahead = FFN_AHEAD
    pending = [up_dots(c) for c in range(ahead)]
    for c in range(nchunks):
        ug, uu = pending.pop(0)
        if c + ahead < nchunks:
            pending.append(up_dots(c + ahead))
        gcol = c * cwid
        gate = conv(ug, gcol)
        up = conv(uu, D_FF + gcol)
        act = (gate * jax.nn.sigmoid(gate) * up).astype(BF16)
        acc = acc + _dot(act, wd_ref[gcol:gcol + cwid, :])
    o_ref[...] = _layer_norm(alpha * x + acc, lg_ref[...], lb_ref[...])


def _ffn_call(alpha, x1, wu, cw, cb, wd, lg, lb, seq):
    m = x1.shape[0]
    tm, hl = FFN_TM, FFN_HALO
    full = _resident_spec
    halo_blocks = tm // hl
    return pl.pallas_call(
        functools.partial(_ffn_kernel, alpha, seq // tm),
        grid=(m // tm,),
        in_specs=[pl.BlockSpec((tm, D_MODEL), lambda i: (i, 0)),
                  pl.BlockSpec((hl, D_MODEL), lambda i: (jnp.maximum(i * halo_blocks - 1, 0), 0)),
                  full(wu), full(cw), full(cb), full(wd), full(lg), full(lb)],
        out_specs=pl.BlockSpec((tm, D_MODEL), lambda i: (i, 0)),
        out_shape=jax.ShapeDtypeStruct((m, D_MODEL), F32),
        compiler_params=pltpu.CompilerParams(
            dimension_semantics=("arbitrary",), vmem_limit_bytes=VMEM_LIMIT),
        name="convffn",
    )(x1, x1, wu, cw, cb, wd, lg, lb)


def kernel(x, positions, w_in, b_gate, sinks, w_branch_a, w_branch_b, w_out, ln1_g, ln1_b,
           w_up, conv_w, conv_b, w_down, ln2_g, ln2_b):
    batch, seq, _ = x.shape
    depth = w_in.shape[0]
    alpha = (2.0 * depth) ** 0.25
    m = batch * seq
    assert seq % SB_TQ == 0 and seq % FFN_TM == 0 and seq % WINDOW == 0
    assert m % PROJ_TM == 0 and m % MERGE_TM == 0

    half = jnp.arange(0, HEAD_DIM, 2, dtype=F32) / HEAD_DIM
    inv_freq = 1.0 / (ROPE_THETA ** half)
    invf = jnp.tile(inv_freq, LANES // (HEAD_DIM // 2))[None, :]
    sign = jnp.tile(jnp.concatenate([-jnp.ones(HEAD_DIM // 2, F32), jnp.ones(HEAD_DIM // 2, F32)]),
                    LANES // HEAD_DIM)[None, :]
    pos2 = positions.reshape(m, 1)
    x2 = x.reshape(m, D_MODEL)

    for l in range(depth):
        qa, ka, kas, va, vas, qb, kb, vb, gates = _proj_call(
            x2, pos2, invf, sign, w_in[l].astype(BF16), b_gate[l][None, :])
        ya = _swa_call(sinks[l], qa, ka, kas, va, vas, batch, seq)
        yb = _sb_call(qb, kb, vb, batch, seq)
        x1 = _merge_call(alpha, x2, ya, yb, gates,
                         w_branch_a[l].astype(BF16), w_branch_b[l].astype(BF16),
                         w_out[l].astype(BF16), ln1_g[l][None, :], ln1_b[l][None, :])
        x2 = _ffn_call(alpha, x1, w_up[l].astype(BF16), conv_w[l], conv_b[l][None, :],
                       w_down[l].astype(BF16), ln2_g[l][None, :], ln2_b[l][None, :], seq)
    return x2.reshape(batch, seq, D_MODEL)
```

```python
import functools
import math

import jax
import jax.numpy as jnp
from jax import lax
from jax.experimental import pallas as pl
from jax.experimental.pallas import tpu as pltpu

D_MODEL = 1024
HEAD_DIM = 64
SWA_Q_HEADS = 8
SWA_KV_HEADS = 2
SB_HEADS = 8
WINDOW = 128
ROPE_THETA = 10000.0
D_FF = 2816
CONV_WIDTH = 3
LN_EPS = 1e-5

SWA_Q_WIDTH = SWA_Q_HEADS * HEAD_DIM
SWA_KV_WIDTH = SWA_KV_HEADS * HEAD_DIM
SB_WIDTH = SB_HEADS * HEAD_DIM
GATE_WIDTH = 2 * D_MODEL
IN_TOTAL = SWA_Q_WIDTH + 2 * SWA_KV_WIDTH + 3 * SB_WIDTH + GATE_WIDTH

LANES = 128
VMEM_LIMIT = 56 * 1024 * 1024

PROJ_TM = 1024
POS_GROUPS = LANES // (HEAD_DIM // 2)
MERGE_TM = 1024
MERGE_SUB = 128
FFN_TM = 512
FFN_CW = 256
FFN_AHEAD = 11
FFN_HALO = 16
SWA_QBLOCKS = 4
SB_TQ = 256
SB_TK = 256
SB_CHUNKS = 4

LOG2E = math.log2(math.e)

F32 = jnp.float32
BF16 = jnp.bfloat16


def _dot(a, b):
    return jnp.dot(a, b, preferred_element_type=F32)


def _dot_nt(a, b):
    return lax.dot_general(a, b, (((1,), (1,)), ((), ())), preferred_element_type=F32)


def _resident_spec(a):
    return pl.BlockSpec(a.shape, lambda i: (0, 0), pipeline_mode=pl.Buffered(1))


def _layer_norm(v, g, b):
    mu = jnp.mean(v, axis=-1, keepdims=True)
    vc = v - mu
    var = jnp.mean(vc * vc, axis=-1, keepdims=True)
    return vc * lax.rsqrt(var + LN_EPS) * g + b


def _proj_kernel(x_ref, pos_ref, invf_ref, sign_ref, w_ref, bg_ref,
                 qa_ref, ka_ref, kas_ref, va_ref, vas_ref, qb_ref, kb_ref, vb_ref, g_ref):
    xb = x_ref[...].astype(BF16)
    lane = lax.broadcasted_iota(jnp.int32, (1, LANES), 1)
    first_half = (lane % HEAD_DIM) < (HEAD_DIM // 2)
    nfreq = HEAD_DIM // 2
    ngroups = POS_GROUPS
    ang =pos_ref[...].astype(F32) * invf_ref[...]
    cos_rolled = [jnp.cos(ang)]
    sin_rolled = [jnp.sin(ang)]
    for s in range(1, ngroups):
        cos_rolled.append(pltpu.roll(cos_rolled[0], nfreq * s, 1))
        sin_rolled.append(pltpu.roll(sin_rolled[0], nfreq * s, 1))
    lane_group = lane // nfreq

    def unpack(rolled, g):
        out = rolled[(0 - g) % ngroups]
        for k in range(1, ngroups):
            out = jnp.where(lane_group == k, rolled[(k - g) % ngroups], out)
        return out

    cosv = jnp.concatenate([unpack(cos_rolled, g) for g in range(ngroups)], axis=0)
    sinv = jnp.concatenate([unpack(sin_rolled, g) for g in range(ngroups)], axis=0) * sign_ref[...]

    def rope(t):
        fwd = pltpu.roll(t, HEAD_DIM // 2, 1)
        bwd = pltpu.roll(t, LANES - HEAD_DIM // 2, 1)
        rot = jnp.where(first_half, bwd, fwd)
        return t * cosv + rot * sinv

    scale = HEAD_DIM ** -0.5
    qa_off = 0
    ka_off = qa_off + SWA_Q_WIDTH
    va_off = ka_off + SWA_KV_WIDTH
    qb_off = va_off + SWA_KV_WIDTH
    kb_off = qb_off + SB_WIDTH
    vb_off = kb_off + SB_WIDTH
    g_off = vb_off + SB_WIDTH
    t = _dot(xb, w_ref[:, qa_off:qa_off + SWA_Q_WIDTH])
    for c in range(SWA_Q_WIDTH // LANES):
        qa_ref[:, c * LANES:(c + 1) * LANES] = (
            rope(t[:, c * LANES:(c + 1) * LANES]) * (scale * LOG2E)).astype(BF16)
    t = rope(_dot(xb, w_ref[:, ka_off:ka_off + SWA_KV_WIDTH]))
    ka_ref[...] = t.astype(BF16)
    kas_ref[...] = pltpu.roll(t, HEAD_DIM, 1).astype(BF16)
    t = _dot(xb, w_ref[:, va_off:va_off + SWA_KV_WIDTH])
    va_ref[...] = t.astype(BF16)
    vas_ref[...] = pltpu.roll(t, HEAD_DIM, 1).astype(BF16)
    qb_ref[...] = (_dot(xb, w_ref[:, qb_off:qb_off + SB_WIDTH]) * (scale * LOG2E)).astype(BF16)
    kb_ref[...] = _dot(xb, w_ref[:, kb_off:kb_off + SB_WIDTH]).astype(BF16)
    vb_ref[...] = _dot(xb, w_ref[:, vb_off:vb_off + SB_WIDTH]).astype(BF16)
    gw = 512
    for c in range(GATE_WIDTH // gw):
        gl = (_dot(xb, w_ref[:, g_off + c * gw:g_off + (c + 1) * gw])
              + bg_ref[:, c * gw:(c + 1) * gw])
        g_ref[:, c * gw:(c + 1) * gw] = jax.nn.sigmoid(gl).astype(BF16)


def _proj_call(x2, pos2, invf, sign, w_in, b_gate):
    m = x2.shape[0]
    tm = PROJ_TM
    row = lambda w: pl.BlockSpec((tm, w), lambda i: (i, 0))
    full = _resident_spec
    out_widths = [SWA_Q_WIDTH, SWA_KV_WIDTH, SWA_KV_WIDTH, SWA_KV_WIDTH, SWA_KV_WIDTH,
                  SB_WIDTH, SB_WIDTH, SB_WIDTH, GATE_WIDTH]
    return pl.pallas_call(
        _proj_kernel,
        grid=(m // tm,),
        in_specs=[row(D_MODEL), pl.BlockSpec((tm // POS_GROUPS, LANES), lambda i: (i, 0)),
                  full(invf), full(sign), full(w_in), full(b_gate)],
        out_specs=[row(w) for w in out_widths],
        out_shape=[jax.ShapeDtypeStruct((m, w), BF16) for w in out_widths],
        compiler_params=pltpu.CompilerParams(
            dimension_semantics=("arbitrary",), vmem_limit_bytes=VMEM_LIMIT),
        name="proj",
    )(x2, pos2, invf, sign, w_in, b_gate)


def _swa_kernel(sinks_ref, q_ref, k_ref, ks_ref, v_ref, vs_ref, o_ref):
    i = pl.program_id(1)
    blk = WINDOW
    lane = lax.broadcasted_iota(jnp.int32, (1, LANES), 1)
    low = lane < HEAD_DIM
    group = SWA_Q_HEADS // SWA_KV_HEADS
    heads = range(SWA_Q_HEADS)
    plain = [(h // group) == (h % 2) for h in heads]
    row_minus_col = (lax.broadcasted_iota(jnp.int32, (blk, 2 * blk), 0)
                     - lax.broadcasted_iota(jnp.int32, (blk, 2 * blk), 1))

    tiles = []
    for u in range(SWA_QBLOCKS):
        qstart = (i * SWA_QBLOCKS + u) * blk
        kstart = pl.multiple_of(jnp.maximum(qstart - blk, 0), blk)
        rel = row_minus_col + (qstart - kstart)
        mask = (rel >= 0) & (rel < WINDOW)
        ksl = pl.ds(kstart, 2 * blk)
        tiles.append((u, mask, (k_ref[ksl, :], ks_ref[ksl, :]), (v_ref[ksl, :], vs_ref[ksl, :])))

    work = [(t, h) for t in tiles for h in heads]
    qms = []
    for (u, _, _, _), h in work:
        qc = q_ref[u * blk:(u + 1) * blk, (h // 2) * LANES:(h // 2 + 1) * LANES]
        qms.append(jnp.where(low if h % 2 == 0 else ~low, qc, jnp.zeros_like(qc)))
    ss = [jnp.where(t[1], _dot_nt(qm, t[2][0] if plain[h] else t[2][1]), -1e30)
          for (t, h), qm in zip(work, qms)]
    sinks2 = [sinks_ref[h] * LOG2E for h in heads]
    mxs = [jnp.maximum(jnp.max(s, axis=-1, keepdims=True), sinks2[h])
           for (_, h), s in zip(work, ss)]
    ps = [jnp.exp2(s - mx) for s, mx in zip(ss, mxs)]
    denoms = [jnp.sum(p, axis=-1, keepdims=True) + jnp.exp2(sinks2[h] - mx)
              for (_, h), p, mx in zip(work, ps, mxs)]
    pvs = [_dot(p.astype(BF16), t[3][0] if plain[h] else t[3][1])
           for (t, h), p in zip(work, ps)]
    outs = [pv / d for pv, d in zip(pvs, denoms)]
    for n in range(0, len(work), 2):
        (u, _, _, _), h = work[n]
        o_ref[u * blk:(u + 1) * blk, (h // 2) * LANES:(h // 2 + 1) * LANES] = jnp.where(
            low, outs[n], outs[n + 1]).astype(BF16)


def _swa_call(sinks, qa, ka, kas, va, vas, batch, seq):
    m = qa.shape[0]
    rows = SWA_QBLOCKS * WINDOW
    n = seq // rows
    qspec = pl.BlockSpec((rows, SWA_Q_WIDTH), lambda b, i: (b * n + i, 0))
    kvspec = pl.BlockSpec((seq, SWA_KV_WIDTH), lambda b, i: (b, 0))
    return pl.pallas_call(
        _swa_kernel,
        grid=(batch, n),
        in_specs=[pl.BlockSpec(memory_space=pltpu.SMEM), qspec, kvspec, kvspec, kvspec, kvspec],
        out_specs=qspec,
        out_shape=jax.ShapeDtypeStruct((m, SWA_Q_WIDTH), BF16),
        compiler_params=pltpu.CompilerParams(
            dimension_semantics=("arbitrary", "arbitrary"), vmem_limit_bytes=VMEM_LIMIT),
        name="swa",
    )(sinks, qa, ka, kas, va, vas)


def _softplus2(z):
    neg_abs = lax.bitcast_convert_type(
        lax.bitcast_convert_type(z, jnp.uint32) | jnp.uint32(0x80000000), F32)
    return jnp.maximum(z, 0.0) + jnp.log2(1.0 + jnp.exp2(neg_abs))


def _sb_kernel(q_ref, k_ref, v_ref, o_ref, acc_ref, carry_ref, factor_ref):
    i = pl.program_id(2)
    tq, tk = SB_TQ, SB_TK
    chunks = q_ref.shape[1] // LANES
    lane = lax.broadcasted_iota(jnp.int32, (1, LANES), 1)
    low = lane < HEAD_DIM
    rr = lax.broadcasted_iota(jnp.int32, (tk, tk), 0)
    cc = lax.broadcasted_iota(jnp.int32, (tk, tk), 1)
    incl_tri = jnp.where(rr >= cc, 1.0, 0.0).astype(BF16)
    heads = range(2 * chunks)
    half = tq // 2

    def tile(j, nrows, diag):
        qms = []
        for c in range(chunks):
            q = q_ref[0:nrows, c * LANES:(c + 1) * LANES]
            qms.append(jnp.where(low, q, jnp.zeros_like(q)))
            qms.append(jnp.where(low, jnp.zeros_like(q), q))
        start = pl.multiple_of(j * tk, tk)
        kts = [k_ref[pl.ds(start, tk), c * LANES:(c + 1) * LANES] for c in range(chunks)]
        vts = [v_ref[pl.ds(start, tk), c * LANES:(c + 1) * LANES] for c in range(chunks)]
        if diag:
            causal = (lax.broadcasted_iota(jnp.int32, (nrows, tk), 1)
                      < lax.broadcasted_iota(jnp.int32, (nrows, tk), 0))
        zs = [_dot_nt(qms[h], kts[h // 2]) for h in heads]
        sps = [_softplus2(z) for z in zs]
        if diag:
            sps = [jnp.where(causal, sp, 0.0) for sp in sps]
        incls = [_dot(sp.astype(BF16), incl_tri) for sp in sps]
        aa = [jnp.exp2(z - incl) for z, incl in zip(zs, incls)]
        if diag:
            aa = [jnp.where(causal, a, 0.0) for a in aa]
        pvs = [_dot(aa[h].astype(BF16), vts[h // 2]) for h in heads]
        sums = [jnp.broadcast_to(incl[:, 0:1], (nrows, LANES)) for incl in incls]
        pv = [jnp.where(low, pvs[2 * c], pvs[2 * c + 1]) for c in range(chunks)]
        mass = [jnp.where(low, sums[2 * c], sums[2 * c + 1]) for c in range(chunks)]
        return pv, mass

    def advance(mass, nrows, first):
        fmax = None
        for c in range(chunks):
            sl = slice(c * LANES, (c + 1) * LANES)
            carry = mass[c] if first else carry_ref[0:nrows, sl] + mass[c]
            carry_ref[0:nrows, sl] = carry
            f = jnp.exp2(-carry)
            factor_ref[0:nrows, sl] = f
            fmax = f if fmax is None else jnp.maximum(fmax, f)
        alive_lo = (jnp.max(fmax[0:half]) > 0.0).astype(jnp.int32)
        if nrows == half:
            return alive_lo
        return alive_lo, (jnp.max(fmax[half:]) > 0.0).astype(jnp.int32)

    pv, mass = tile(i, tq, True)
    for c in range(chunks):
        acc_ref[:, c * LANES:(c + 1) * LANES] = pv[c]
    alive_lo, alive_hi = advance(mass, tq, True)

    def update(j, nrows):
        pv, mass = tile(j, nrows, False)
        for c in range(chunks):
            sl = slice(c * LANES, (c + 1) * LANES)
            acc_ref[0:nrows, sl] += factor_ref[0:nrows, sl] * pv[c]
        return advance(mass, nrows, False)

    def cond(state):
        n, alive_lo, alive_hi = state
        return (n < i) & ((alive_lo | alive_hi) > 0)

    def body(state):
        n, _, alive_hi = state
        j = i - 1 - n
        alive_lo, alive_hi = lax.cond(
            alive_hi > 0,
            lambda: update(j, tq),
            lambda: (update(j, half), jnp.int32(0)))
        return n + 1, alive_lo, alive_hi

    lax.while_loop(cond, body, (jnp.int32(0), alive_lo, alive_hi))
    o_ref[...] = acc_ref[...].astype(BF16)


def _sb_call(qb, kb, vb, batch, seq):
    m = qb.shape[0]
    tq = SB_TQ
    nq = seq // tq
    width = SB_CHUNKS * LANES
    groups = SB_WIDTH // width
    return pl.pallas_call(
        _sb_kernel,
        grid=(batch, groups, nq),
        in_specs=[pl.BlockSpec((tq, width), lambda b, p, i: (b * nq + i, p)),
                  pl.BlockSpec((seq, width), lambda b, p, i: (b, p)),
                  pl.BlockSpec((seq, width), lambda b, p, i: (b, p))],
        out_specs=pl.BlockSpec((tq, width), lambda b, p, i: (b * nq + i, p)),
        out_shape=jax.ShapeDtypeStruct((m, SB_WIDTH), BF16),
        scratch_shapes=[pltpu.VMEM((tq, width), F32)] * 3,
        compiler_params=pltpu.CompilerParams(
            dimension_semantics=("arbitrary", "arbitrary", "arbitrary"),
            vmem_limit_bytes=VMEM_LIMIT),
        name="stickbreak",
    )(qb, kb, vb)


def _merge_kernel(alpha, x_ref, ya_ref, yb_ref, g_ref, wa_ref, wb_ref, wo_ref, lg_ref, lb_ref, o_ref):
    sub = MERGE_SUB
    rows = [slice(r, r + sub) for r in range(0, MERGE_TM, sub)]
    pas = [_dot(ya_ref[r, :], wa_ref[...]) for r in rows]
    pbs = [_dot(yb_ref[r, :], wb_ref[...]) for r in rows]
    hs = [(g_ref[r, 0:D_MODEL].astype(F32) * pa
           + g_ref[r, D_MODEL:2 * D_MODEL].astype(F32) * pb).astype(BF16)
          for r, pa, pb in zip(rows, pas, pbs)]
    res = [alpha * x_ref[r, :] + _dot(h, wo_ref[...]) for r, h in zip(rows, hs)]
    for r, v in zip(rows, res):
        o_ref[r, :] = _layer_norm(v, lg_ref[...], lb_ref[...])


def _merge_call(alpha, x2, ya, yb, gates, wa, wb, wo, lg, lb):
    m = x2.shape[0]
    tm = MERGE_TM
    row = lambda w: pl.BlockSpec((tm, w), lambda i: (i, 0))
    full = _resident_spec
    return pl.pallas_call(
        functools.partial(_merge_kernel, alpha),
        grid=(m // tm,),
        in_specs=[row(D_MODEL), row(SWA_Q_WIDTH), row(SB_WIDTH), row(GATE_WIDTH),
                  full(wa), full(wb), full(wo), full(lg), full(lb)],
        out_specs=row(D_MODEL),
        out_shape=jax.ShapeDtypeStruct((m, D_MODEL), F32),
        compiler_params=pltpu.CompilerParams(
            dimension_semantics=("arbitrary",), vmem_limit_bytes=VMEM_LIMIT),
        name="merge",
    )(x2, ya, yb, gates, wa, wb, wo, lg, lb)


def _ffn_kernel(alpha, tiles_per_seq, x_ref, halo_ref, wu_ref, cw_ref, cb_ref, wd_ref,
                lg_ref, lb_ref, o_ref):
    i = pl.program_id(0)
    tm, hl, cwid = FFN_TM, FFN_HALO, FFN_CW
    x = x_ref[...]
    seq_start = (i % tiles_per_seq) == 0
    halo = jnp.where(seq_start, 0.0, halo_ref[...])
    xe = jnp.concatenate([halo.astype(BF16), x.astype(BF16)], axis=0)

    def conv(u, col):
        sl = slice(col, col + cwid)
        y = (cw_ref[2:3, sl] * u[hl:, :]
             + cw_ref[1:2, sl] * pltpu.roll(u, 1, 0)[hl:, :]
             + cw_ref[0:1, sl] * pltpu.roll(u, 2, 0)[hl:, :])
        return y + cb_ref[:, sl]

    def up_dots(c):
        gcol = c * cwid
        ucol = D_FF + c * cwid
        return _dot(xe, wu_ref[:, gcol:gcol + cwid]), _dot(xe, wu_ref[:, ucol:ucol + cwid])

    nchunks = D_FF // cwid
    acc = jnp.zeros((tm, D_MODEL), F32)
    ahead = FFN_AHEAD
    pending = [up_dots(c) for c in range(ahead)]
    for c in range(nchunks):
        ug, uu = pending.pop(0)
        if c + ahead < nchunks:
            pending.append(up_dots(c + ahead))
        gcol = c * cwid
        gate = conv(ug, gcol)
        up = conv(uu, D_FF + gcol)
        act = (gate * jax.nn.sigmoid(gate) * up).astype(BF16)
        acc = acc + _dot(act, wd_ref[gcol:gcol + cwid, :])
    o_ref[...] = _layer_norm(alpha * x + acc, lg_ref[...], lb_ref[...])


def _ffn_call(alpha, x1, wu, cw, cb, wd, lg, lb, seq):
    m = x1.shape[0]
    tm, hl = FFN_TM, FFN_HALO
    full = _resident_spec
    halo_blocks = tm // hl
    return pl.pallas_call(
        functools.partial(_ffn_kernel, alpha, seq // tm),
        grid=(m // tm,),
        in_specs=[pl.BlockSpec((tm, D_MODEL), lambda i: (i, 0)),
                  pl.BlockSpec((hl, D_MODEL), lambda i: (jnp.maximum(i * halo_blocks - 1, 0), 0)),
                  full(wu), full(cw), full(cb), full(wd), full(lg), full(lb)],
        out_specs=pl.BlockSpec((tm, D_MODEL), lambda i: (i, 0)),
        out_shape=jax.ShapeDtypeStruct((m, D_MODEL), F32),
        compiler_params=pltpu.CompilerParams(
            dimension_semantics=("arbitrary",), vmem_limit_bytes=VMEM_LIMIT),
        name="convffn",
    )(x1, x1, wu, cw, cb, wd, lg, lb)


def kernel(x, positions, w_in, b_gate, sinks, w_branch_a, w_branch_b, w_out, ln1_g, ln1_b,
           w_up, conv_w, conv_b, w_down, ln2_g, ln2_b):
    batch, seq, _ = x.shape
    depth = w_in.shape[0]
    alpha = (2.0 * depth) ** 0.25
    m = batch * seq
    assert seq % SB_TQ == 0 and seq % FFN_TM == 0 and seq % WINDOW == 0
    assert m % PROJ_TM == 0 and m % MERGE_TM == 0

    half = jnp.arange(0, HEAD_DIM, 2, dtype=F32) / HEAD_DIM
    inv_freq = 1.0 / (ROPE_THETA ** half)
    invf = jnp.tile(inv_freq, LANES // (HEAD_DIM // 2))[None, :]
    sign = jnp.tile(jnp.concatenate([-jnp.ones(HEAD_DIM // 2, F32), jnp.ones(HEAD_DIM // 2, F32)]),
                    LANES // HEAD_DIM)[None, :]
    slab = PROJ_TM // POS_GROUPS
    pos2 = positions.reshape(m // PROJ_TM, POS_GROUPS, slab).transpose(0, 2, 1)
    pos2 = jnp.repeat(pos2, LANES // POS_GROUPS, axis=2).reshape(m // POS_GROUPS, LANES)
    x2 = x.reshape(m, D_MODEL)

    for l in range(depth):
        qa, ka, kas, va, vas, qb, kb, vb, gates = _proj_call(
            x2, pos2, invf, sign, w_in[l].astype(BF16), b_gate[l][None, :])
        ya = _swa_call(sinks[l], qa, ka, kas, va, vas, batch, seq)
        yb = _sb_call(qb, kb, vb, batch, seq)
        x1 = _merge_call(alpha, x2, ya, yb, gates,
                         w_branch_a[l].astype(BF16), w_branch_b[l].astype(BF16),
                         w_out[l].astype(BF16), ln1_g[l][None, :], ln1_b[l][None, :])
        x2 = _ffn_call(alpha, x1, w_up[l].astype(BF16), conv_w[l], conv_b[l][None, :],
                       w_down[l].astype(BF16), ln2_g[l][None, :], ln2_b[l][None, :], seq)
    return x2.reshape(batch, seq, D_MODEL)
```

```python
import functools
import math

import jax
import jax.numpy as jnp
from jax import lax
from jax.experimental import pallas as pl
from jax.experimental.pallas import tpu as pltpu

D_MODEL = 1024
HEAD_DIM = 64
SWA_Q_HEADS = 8
SWA_KV_HEADS = 2
SB_HEADS = 8
WINDOW = 128
ROPE_THETA = 10000.0
D_FF = 2816
CONV_WIDTH = 3
LN_EPS = 1e-5

SWA_Q_WIDTH = SWA_Q_HEADS * HEAD_DIM
SWA_KV_WIDTH = SWA_KV_HEADS * HEAD_DIM
SB_WIDTH = SB_HEADS * HEAD_DIM
GATE_WIDTH = 2 * D_MODEL
IN_TOTAL = SWA_Q_WIDTH + 2 * SWA_KV_WIDTH + 3 * SB_WIDTH + GATE_WIDTH

LANES = 128
VMEM_LIMIT = 56 * 1024 * 1024

PROJ_TM = 1024
POS_GROUPS = LANES // (HEAD_DIM // 2)
MERGE_TM = 1024
MERGE_SUB = 128
FFN_TM = 512
FFN_CW = 256
FFN_AHEAD = 11
FFN_HALO = 16
SWA_QBLOCKS = 2
SB_TQ = 256
SB_TK = 256
SB_CHUNKS = 4

LOG2E = math.log2(math.e)

F32 = jnp.float32
BF16 = jnp.bfloat16


def _dot(a, b):
    return jnp.dot(a, b, preferred_element_type=F32)


def _dot_nt(a, b):
    return lax.dot_general(a, b, (((1,), (1,)), ((), ())), preferred_element_type=F32)


def _resident_spec(a):
    return pl.BlockSpec(a.shape, lambda i: (0, 0), pipeline_mode=pl.Buffered(1))


def _layer_norm(v, g, b):
    mu = jnp.mean(v, axis=-1, keepdims=True)
    vc = v - mu
    var = jnp.mean(vc * vc, axis=-1, keepdims=True)
    return vc * lax.rsqrt(var + LN_EPS) * g + b


def _proj_kernel(x_ref, pos_ref, invf_ref, sign_ref, w_ref, bg_ref,
                 qa_ref, ka_ref, kas_ref, va_ref, vas_ref, qb_ref, kb_ref, vb_ref, g_ref):
    xb = x_ref[...].astype(BF16)
    lane = lax.broadcasted_iota(jnp.int32, (1, LANES), 1)
    first_half = (lane % HEAD_DIM) < (HEAD_DIM // 2)
    nfreq = HEAD_DIM // 2
    ngroups = POS_GROUPS
    ang =pos_ref[...].astype(F32) * invf_ref[...]
    cos_rolled = [jnp.cos(ang)]
    sin_rolled = [jnp.sin(ang)]
    for s in range(1, ngroups):
        cos_rolled.append(pltpu.roll(cos_rolled[0], nfreq * s, 1))
        sin_rolled.append(pltpu.roll(sin_rolled[0], nfreq * s, 1))
    lane_group = lane // nfreq

    def unpack(rolled, g):
        out = rolled[(0 - g) % ngroups]
        for k in range(1, ngroups):
            out = jnp.where(lane_group == k, rolled[(k - g) % ngroups], out)
        return out

    cosv = jnp.concatenate([unpack(cos_rolled, g) for g in range(ngroups)], axis=0)
    sinv = jnp.concatenate([unpack(sin_rolled, g) for g in range(ngroups)], axis=0) * sign_ref[...]

    def rope(t):
        fwd = pltpu.roll(t, HEAD_DIM // 2, 1)
        bwd = pltpu.roll(t, LANES - HEAD_DIM // 2, 1)
        rot = jnp.where(first_half, bwd, fwd)
        return t * cosv + rot * sinv

    scale = HEAD_DIM ** -0.5
    qa_off = 0
    ka_off = qa_off + SWA_Q_WIDTH
    va_off = ka_off + SWA_KV_WIDTH
    qb_off = va_off + SWA_KV_WIDTH
    kb_off = qb_off + SB_WIDTH
    vb_off = kb_off + SB_WIDTH
    g_off = vb_off + SB_WIDTH
    t = _dot(xb, w_ref[:, qa_off:qa_off + SWA_Q_WIDTH])
    for c in range(SWA_Q_WIDTH // LANES):
        qa_ref[:, c * LANES:(c + 1) * LANES] = (
            rope(t[:, c * LANES:(c + 1) * LANES]) * (scale * LOG2E)).astype(BF16)
    t = rope(_dot(xb, w_ref[:, ka_off:ka_off + SWA_KV_WIDTH]))
    ka_ref[...] = t.astype(BF16)
    kas_ref[...] = pltpu.roll(t, HEAD_DIM, 1).astype(BF16)
    t = _dot(xb, w_ref[:, va_off:va_off + SWA_KV_WIDTH])
    va_ref[...] = t.astype(BF16)
    vas_ref[...] = pltpu.roll(t, HEAD_DIM, 1).astype(BF16)
    qb_ref[...] = (_dot(xb, w_ref[:, qb_off:qb_off + SB_WIDTH]) * (scale * LOG2E)).astype(BF16)
    kb_ref[...] = _dot(xb, w_ref[:, kb_off:kb_off + SB_WIDTH]).astype(BF16)
    vb_ref[...] = _dot(xb, w_ref[:, vb_off:vb_off + SB_WIDTH]).astype(BF16)
    gw = 512
    for c in range(GATE_WIDTH // gw):
        gl = (_dot(xb, w_ref[:, g_off + c * gw:g_off + (c + 1) * gw])
              + bg_ref[:, c * gw:(c + 1) * gw])
        g_ref[:, c * gw:(c + 1) * gw] = jax.nn.sigmoid(gl).astype(BF16)


def _proj_call(x2, pos2, invf, sign, w_in, b_gate):
    m = x2.shape[0]
    tm = PROJ_TM
    row = lambda w: pl.BlockSpec((tm, w), lambda i: (i, 0))
    full = _resident_spec
    out_widths = [SWA_Q_WIDTH, SWA_KV_WIDTH, SWA_KV_WIDTH, SWA_KV_WIDTH, SWA_KV_WIDTH,
                  SB_WIDTH, SB_WIDTH, SB_WIDTH, GATE_WIDTH]
    return pl.pallas_call(
        _proj_kernel,
        grid=(m // tm,),
        in_specs=[row(D_MODEL), pl.BlockSpec((tm // POS_GROUPS, LANES), lambda i: (i, 0)),
                  full(invf), full(sign), full(w_in), full(b_gate)],
        out_specs=[row(w) for w in out_widths],
        out_shape=[jax.ShapeDtypeStruct((m, w), BF16) for w in out_widths],
        compiler_params=pltpu.CompilerParams(
            dimension_semantics=("arbitrary",), vmem_limit_bytes=VMEM_LIMIT),
        name="proj",
    )(x2, pos2, invf, sign, w_in, b_gate)


def _swa_kernel(sinks_ref, q_ref, k_ref, ks_ref, v_ref, vs_ref, o_ref):
    blk = WINDOW
    lane = lax.broadcasted_iota(jnp.int32, (1, LANES), 1)
    low = lane < HEAD_DIM
    group = SWA_Q_HEADS // SWA_KV_HEADS
    heads = range(SWA_Q_HEADS)
    plain = [(h // group) == (h % 2) for h in heads]
    row_minus_col = (lax.broadcasted_iota(jnp.int32, (blk, 2 * blk), 0)
                     - lax.broadcasted_iota(jnp.int32, (blk, 2 * blk), 1))

    sinks2 = [sinks_ref[h] * LOG2E for h in heads]

    @pl.loop(0, q_ref.shape[0] // (SWA_QBLOCKS * blk))
    def _(i):
        tiles = []
        for u in range(SWA_QBLOCKS):
            qstart = pl.multiple_of((i * SWA_QBLOCKS + u) * blk, blk)
            kstart = pl.multiple_of(jnp.maximum(qstart - blk, 0), blk)
            rel = row_minus_col + (qstart - kstart)
            mask = (rel >= 0) & (rel < WINDOW)
            ksl = pl.ds(kstart, 2 * blk)
            tiles.append((pl.ds(qstart, blk), mask, (k_ref[ksl, :], ks_ref[ksl, :]),
                          (v_ref[ksl, :], vs_ref[ksl, :])))

        work = [(t, h) for t in tiles for h in heads]
        qms = []
        for (qrows, _, _, _), h in work:
            qc = q_ref[qrows, (h // 2) * LANES:(h // 2 + 1) * LANES]
            qms.append(jnp.where(low if h % 2 == 0 else ~low, qc, jnp.zeros_like(qc)))
        ss = [jnp.where(t[1], _dot_nt(qm, t[2][0] if plain[h] else t[2][1]), -1e30)
              for (t, h), qm in zip(work, qms)]
        mxs = [jnp.maximum(jnp.max(s, axis=-1, keepdims=True), sinks2[h])
               for (_, h), s in zip(work, ss)]
        ps = [jnp.exp2(s - mx) for s, mx in zip(ss, mxs)]
        denoms = [jnp.sum(p, axis=-1, keepdims=True) + jnp.exp2(sinks2[h] - mx)
                  for (_, h), p, mx in zip(work, ps, mxs)]
        pvs = [_dot(p.astype(BF16), t[3][0] if plain[h] else t[3][1])
               for (t, h), p in zip(work, ps)]
        outs = [pv / d for pv, d in zip(pvs, denoms)]
        for n in range(0, len(work), 2):
            (qrows, _, _, _), h = work[n]
            o_ref[qrows, (h // 2) * LANES:(h // 2 + 1) * LANES] = jnp.where(
                low, outs[n], outs[n + 1]).astype(BF16)


def _swa_call(sinks, qa, ka, kas, va, vas, batch, seq):
    m = qa.shape[0]
    qspec = pl.BlockSpec((seq, SWA_Q_WIDTH), lambda b: (b, 0))
    kvspec = pl.BlockSpec((seq, SWA_KV_WIDTH), lambda b: (b, 0))
    return pl.pallas_call(
        _swa_kernel,
        grid=(batch,),
        in_specs=[pl.BlockSpec(memory_space=pltpu.SMEM), qspec, kvspec, kvspec, kvspec, kvspec],
        out_specs=qspec,
        out_shape=jax.ShapeDtypeStruct((m, SWA_Q_WIDTH), BF16),
        compiler_params=pltpu.CompilerParams(
            dimension_semantics=("arbitrary",), vmem_limit_bytes=VMEM_LIMIT),
        name="swa",
    )(sinks, qa, ka, kas, va, vas)


def _softplus2(z):
    neg_abs = lax.bitcast_convert_type(
        lax.bitcast_convert_type(z, jnp.uint32) | jnp.uint32(0x80000000), F32)
    return jnp.maximum(z, 0.0) + jnp.log2(1.0 + jnp.exp2(neg_abs))


def _sb_kernel(q_ref, k_ref, v_ref, o_ref, acc_ref, carry_ref, factor_ref):
    tq, tk = SB_TQ, SB_TK
    chunks = q_ref.shape[1] // LANES
    lane = lax.broadcasted_iota(jnp.int32, (1, LANES), 1)
    low = lane < HEAD_DIM
    rr = lax.broadcasted_iota(jnp.int32, (tk, tk), 0)
    cc = lax.broadcasted_iota(jnp.int32, (tk, tk), 1)
    incl_tri = jnp.where(rr >= cc, 1.0, 0.0).astype(BF16)
    heads = range(2 * chunks)
    half = tq // 2

    def tile(row0, j, nrows, diag):
        qms = []
        for c in range(chunks):
            q = q_ref[pl.ds(row0, nrows), c * LANES:(c + 1) * LANES]
            qms.append(jnp.where(low, q, jnp.zeros_like(q)))
            qms.append(jnp.where(low, jnp.zeros_like(q), q))
        start = pl.multiple_of(j * tk, tk)
        kts = [k_ref[pl.ds(start, tk), c * LANES:(c + 1) * LANES] for c in range(chunks)]
        vts = [v_ref[pl.ds(start, tk), c * LANES:(c + 1) * LANES] for c in range(chunks)]
        if diag:
            causal = (lax.broadcasted_iota(jnp.int32, (nrows, tk), 1)
                      < lax.broadcasted_iota(jnp.int32, (nrows, tk), 0))
        zs = [_dot_nt(qms[h], kts[h // 2]) for h in heads]
        sps = [_softplus2(z) for z in zs]
        if diag:
            sps = [jnp.where(causal, sp, 0.0) for sp in sps]
        incls = [_dot(sp.astype(BF16), incl_tri) for sp in sps]
        aa = [jnp.exp2(z - incl) for z, incl in zip(zs, incls)]
        if diag:
            aa = [jnp.where(causal, a, 0.0) for a in aa]
        pvs = [_dot(aa[h].astype(BF16), vts[h // 2]) for h in heads]
        sums = [jnp.broadcast_to(incl[:, 0:1], (nrows, LANES)) for incl in incls]
        pv = [jnp.where(low, pvs[2 * c], pvs[2 * c + 1]) for c in range(chunks)]
        mass = [jnp.where(low, sums[2 * c], sums[2 * c + 1]) for c in range(chunks)]
        return pv, mass

    def advance(mass, nrows, first):
        fmax = None
        for c in range(chunks):
            sl = slice(c * LANES, (c + 1) * LANES)
            carry = mass[c] if first else carry_ref[0:nrows, sl] + mass[c]
            carry_ref[0:nrows, sl] = carry
            f = jnp.exp2(-carry)
            factor_ref[0:nrows, sl] = f
            fmax = f if fmax is None else jnp.maximum(fmax, f)
        alive_lo = (jnp.max(fmax[0:half]) > 0.0).astype(jnp.int32)
        if nrows == half:
            return alive_lo
        return alive_lo, (jnp.max(fmax[half:]) > 0.0).astype(jnp.int32)

    def update(row0, j, nrows):
        pv, mass = tile(row0, j, nrows, False)
        for c in range(chunks):
            sl = slice(c * LANES, (c + 1) * LANES)
            acc_ref[0:nrows, sl] += factor_ref[0:nrows, sl] * pv[c]
        return advance(mass, nrows, False)

    @pl.loop(0, q_ref.shape[0] // tq)
    def _(i):
        row0 = pl.multiple_of(i * tq, tq)
        pv, mass = tile(row0, i, tq, True)
        for c in range(chunks):
            acc_ref[:, c * LANES:(c + 1) * LANES] = pv[c]
        alive_lo, alive_hi = advance(mass, tq, True)

        def cond(state):
            n, alive_lo, alive_hi = state
            return (n < i) & ((alive_lo | alive_hi) > 0)

        def body(state):
            n, _, alive_hi = state
            j = i - 1 - n
            alive_lo, alive_hi = lax.cond(
                alive_hi > 0,
                lambda: update(row0, j, tq),
                lambda: (update(row0, j, half), jnp.int32(0)))
            return n + 1, alive_lo, alive_hi

        lax.while_loop(cond, body, (jnp.int32(0), alive_lo, alive_hi))
        o_ref[pl.ds(row0, tq), :] = acc_ref[...].astype(BF16)


def _sb_call(qb, kb, vb, batch, seq):
    m = qb.shape[0]
    width = SB_CHUNKS * LANES
    groups = SB_WIDTH // width
    spec = pl.BlockSpec((seq, width), lambda b, p: (b, p))
    return pl.pallas_call(
        _sb_kernel,
        grid=(batch, groups),
        in_specs=[spec, spec, spec],
        out_specs=spec,
        out_shape=jax.ShapeDtypeStruct((m, SB_WIDTH), BF16),
        scratch_shapes=[pltpu.VMEM((SB_TQ, width), F32)] * 3,
        compiler_params=pltpu.CompilerParams(
            dimension_semantics=("arbitrary", "arbitrary"),
            vmem_limit_bytes=VMEM_LIMIT),
        name="stickbreak",
    )(qb, kb, vb)


def _merge_kernel(alpha, x_ref, ya_ref, yb_ref, g_ref, wa_ref, wb_ref, wo_ref, lg_ref, lb_ref, o_ref):
    sub = MERGE_SUB
    rows = [slice(r, r + sub) for r in range(0, MERGE_TM, sub)]
    pas = [_dot(ya_ref[r, :], wa_ref[...]) for r in rows]
    pbs = [_dot(yb_ref[r, :], wb_ref[...]) for r in rows]
    hs = [(g_ref[r, 0:D_MODEL].astype(F32) * pa
           + g_ref[r, D_MODEL:2 * D_MODEL].astype(F32) * pb).astype(BF16)
          for r, pa, pb in zip(rows, pas, pbs)]
    res = [alpha * x_ref[r, :] + _dot(h, wo_ref[...]) for r, h in zip(rows, hs)]
    for r, v in zip(rows, res):
        o_ref[r, :] = _layer_norm(v, lg_ref[...], lb_ref[...])


def _merge_call(alpha, x2, ya, yb, gates, wa, wb, wo, lg, lb):
    m = x2.shape[0]
    tm = MERGE_TM
    row = lambda w: pl.BlockSpec((tm, w), lambda i: (i, 0))
    full = _resident_spec
    return pl.pallas_call(
        functools.partial(_merge_kernel, alpha),
        grid=(m // tm,),
        in_specs=[row(D_MODEL), row(SWA_Q_WIDTH), row(SB_WIDTH), row(GATE_WIDTH),
                  full(wa), full(wb), full(wo), full(lg), full(lb)],
        out_specs=row(D_MODEL),
        out_shape=jax.ShapeDtypeStruct((m, D_MODEL), F32),
        compiler_params=pltpu.CompilerParams(
            dimension_semantics=("arbitrary",), vmem_limit_bytes=VMEM_LIMIT),
        name="merge",
    )(x2, ya, yb, gates, wa, wb, wo, lg, lb)


def _ffn_kernel(alpha, tiles_per_seq, x_ref, halo_ref, wu_ref, cw_ref, cb_ref, wd_ref,
                lg_ref, lb_ref, o_ref):
    i = pl.program_id(0)
    tm, hl, cwid = FFN_TM, FFN_HALO, FFN_CW
    x = x_ref[...]
    seq_start = (i % tiles_per_seq) == 0
    halo = jnp.where(seq_start, 0.0, halo_ref[...])
    xe = jnp.concatenate([halo.astype(BF16), x.astype(BF16)], axis=0)

    def conv(u, col):
        sl = slice(col, col + cwid)
        y = (cw_ref[2:3, sl] * u[hl:, :]
             + cw_ref[1:2, sl] * pltpu.roll(u, 1, 0)[hl:, :]
             + cw_ref[0:1, sl] * pltpu.roll(u, 2, 0)[hl:, :])
        return y + cb_ref[:, sl]

    def up_dots(c):
        gcol = c * cwid
        ucol = D_FF + c * cwid
        return _dot(xe, wu_ref[:, gcol:gcol + cwid]), _dot(xe, wu_ref[:, ucol:ucol + cwid])

    nchunks = D_FF // cwid
    acc = jnp.zeros((tm, D_MODEL), F32)
    ahead = FFN_AHEAD
    pending = [up_dots(c) for c in range(ahead)]
    for c in range(nchunks):
        ug, uu = pending.pop(0)
        if c + ahead < nchunks:
            pending.append(up_dots(c + ahead))
        gcol = c * cwid
        gate = conv(ug, gcol)
        up = conv(uu, D_FF + gcol)
        act = (gate * jax.nn.sigmoid(gate) * up).astype(BF16)
        acc = acc + _dot(act, wd_ref[gcol:gcol + cwid, :])
    o_ref[...] = _layer_norm(alpha * x + acc, lg_ref[...], lb_ref[...])


def _ffn_call(alpha, x1, wu, cw, cb, wd, lg, lb, seq):
    m = x1.shape[0]
    tm, hl = FFN_TM, FFN_HALO
    full = _resident_spec
    halo_blocks = tm // hl
    return pl.pallas_call(
        functools.partial(_ffn_kernel, alpha, seq // tm),
        grid=(m // tm,),
        in_specs=[pl.BlockSpec((tm, D_MODEL), lambda i: (i, 0)),
                  pl.BlockSpec((hl, D_MODEL), lambda i: (jnp.maximum(i * halo_blocks - 1, 0), 0)),
                  full(wu), full(cw), full(cb), full(wd), full(lg), full(lb)],
        out_specs=pl.BlockSpec((tm, D_MODEL), lambda i: (i, 0)),
        out_shape=jax.ShapeDtypeStruct((m, D_MODEL), F32),
        compiler_params=pltpu.CompilerParams(
            dimension_semantics=("arbitrary",), vmem_limit_bytes=VMEM_LIMIT),
        name="convffn",
    )(x1, x1, wu, cw, cb, wd, lg, lb)


def kernel(x, positions, w_in, b_gate, sinks, w_branch_a, w_branch_b, w_out, ln1_g, ln1_b,
           w_up, conv_w, conv_b, w_down, ln2_g, ln2_b):
    batch, seq, _ = x.shape
    depth = w_in.shape[0]
    alpha = (2.0 * depth) ** 0.25
    m = batch * seq
    assert seq % SB_TQ == 0 and seq % FFN_TM == 0 and seq % WINDOW == 0
    assert m % PROJ_TM == 0 and m % MERGE_TM == 0

    half = jnp.arange(0, HEAD_DIM, 2, dtype=F32) / HEAD_DIM
    inv_freq = 1.0 / (ROPE_THETA ** half)
    invf = jnp.tile(inv_freq, LANES // (HEAD_DIM // 2))[None, :]
    sign = jnp.tile(jnp.concatenate([-jnp.ones(HEAD_DIM // 2, F32), jnp.ones(HEAD_DIM // 2, F32)]),
                    LANES // HEAD_DIM)[None, :]
    slab = PROJ_TM // POS_GROUPS
    pos2 = positions.reshape(m // PROJ_TM, POS_GROUPS, slab).transpose(0, 2, 1)
    pos2 = jnp.repeat(pos2, LANES // POS_GROUPS, axis=2).reshape(m // POS_GROUPS, LANES)
    x2 = x.reshape(m, D_MODEL)

    for l in range(depth):
        qa, ka, kas, va, vas, qb, kb, vb, gates = _proj_call(
            x2, pos2, invf, sign, w_in[l].astype(BF16), b_gate[l][None, :])
        ya = _swa_call(sinks[l], qa, ka, kas, va, vas, batch, seq)
        yb = _sb_call(qb, kb, vb, batch, seq)
        x1 = _merge_call(alpha, x2, ya, yb, gates,
                         w_branch_a[l].astype(BF16), w_branch_b[l].astype(BF16),
                         w_out[l].astype(BF16), ln1_g[l][None, :], ln1_b[l][None, :])
        x2 = _ffn_call(alpha, x1, w_up[l].astype(BF16), conv_w[l], conv_b[l][None, :],
                       w_down[l].astype(BF16), ln2_g[l][None, :], ln2_b[l][None, :], seq)
    return x2.reshape(batch, seq, D_MODEL)
```

```python
import functools
import math

import jax
import jax.numpy as jnp
from jax import lax
from jax.experimental import pallas as pl
from jax.experimental.pallas import tpu as pltpu

D_MODEL = 1024
HEAD_DIM = 64
SWA_Q_HEADS = 8
SWA_KV_HEADS = 2
SB_HEADS = 8
WINDOW = 128
ROPE_THETA = 10000.0
D_FF = 2816
CONV_WIDTH = 3
LN_EPS = 1e-5

SWA_Q_WIDTH = SWA_Q_HEADS * HEAD_DIM
SWA_KV_WIDTH = SWA_KV_HEADS * HEAD_DIM
SB_WIDTH = SB_HEADS * HEAD_DIM
GATE_WIDTH = 2 * D_MODEL
IN_TOTAL = SWA_Q_WIDTH + 2 * SWA_KV_WIDTH + 3 * SB_WIDTH + GATE_WIDTH

LANES = 128
VMEM_LIMIT = 56 * 1024 * 1024

PROJ_TM = 1024
POS_GROUPS = LANES // (HEAD_DIM // 2)
MERGE_TM = 1024
MERGE_SUB = 128
FFN_TM = 1024
FFN_CW = 256
FFN_AHEAD = 2
FFN_DOWN_ROWS = 256
FFN_HALO = 16
SWA_QBLOCKS = 2
SB_TQ = 256
SB_TK = 256
SB_CHUNKS = 4

LOG2E = math.log2(math.e)

F32 = jnp.float32
BF16 = jnp.bfloat16


def _dot(a, b):
    return jnp.dot(a, b, preferred_element_type=F32)


def _dot_nt(a, b):
    return lax.dot_general(a, b, (((1,), (1,)), ((), ())), preferred_element_type=F32)


def _resident_spec(a):
    return pl.BlockSpec(a.shape, lambda i: (0, 0), pipeline_mode=pl.Buffered(1))


def _layer_norm(v, g, b):
    mu = jnp.mean(v, axis=-1, keepdims=True)
    vc = v - mu
    var = jnp.mean(vc * vc, axis=-1, keepdims=True)
    return vc * lax.rsqrt(var + LN_EPS) * g + b


def _proj_kernel(x_ref, pos_ref, invf_ref, sign_ref, w_ref, bg_ref,
                 qa_ref, ka_ref, kas_ref, va_ref, vas_ref, qb_ref, kb_ref, vb_ref, g_ref):
    xb = x_ref[...].astype(BF16)
    lane = lax.broadcasted_iota(jnp.int32, (1, LANES), 1)
    first_half = (lane % HEAD_DIM) < (HEAD_DIM // 2)
    nfreq = HEAD_DIM // 2
    ngroups = POS_GROUPS
    ang =pos_ref[...].astype(F32) * invf_ref[...]
    cos_rolled = [jnp.cos(ang)]
    sin_rolled = [jnp.sin(ang)]
    for s in range(1, ngroups):
        cos_rolled.append(pltpu.roll(cos_rolled[0], nfreq * s, 1))
        sin_rolled.append(pltpu.roll(sin_rolled[0], nfreq * s, 1))
    lane_group = lane // nfreq

    def unpack(rolled, g):
        out = rolled[(0 - g) % ngroups]
        for k in range(1, ngroups):
            out = jnp.where(lane_group == k, rolled[(k - g) % ngroups], out)
        return out

    cosv = jnp.concatenate([unpack(cos_rolled, g) for g in range(ngroups)], axis=0)
    sinv = jnp.concatenate([unpack(sin_rolled, g) for g in range(ngroups)], axis=0) * sign_ref[...]

    def rope(t):
        fwd = pltpu.roll(t, HEAD_DIM // 2, 1)
        bwd = pltpu.roll(t, LANES - HEAD_DIM // 2, 1)
        rot = jnp.where(first_half, bwd, fwd)
        return t * cosv + rot * sinv

    scale = HEAD_DIM ** -0.5
    qa_off = 0
    ka_off = qa_off + SWA_Q_WIDTH
    va_off = ka_off + SWA_KV_WIDTH
    qb_off = va_off + SWA_KV_WIDTH
    kb_off = qb_off + SB_WIDTH
    vb_off = kb_off + SB_WIDTH
    g_off = vb_off + SB_WIDTH
    t = _dot(xb, w_ref[:, qa_off:qa_off + SWA_Q_WIDTH])
    for c in range(SWA_Q_WIDTH // LANES):
        qa_ref[:, c * LANES:(c + 1) * LANES] = (
            rope(t[:, c * LANES:(c + 1) * LANES]) * (scale * LOG2E)).astype(BF16)
    t = rope(_dot(xb, w_ref[:, ka_off:ka_off + SWA_KV_WIDTH]))
    ka_ref[...] = t.astype(BF16)
    kas_ref[...] = pltpu.roll(t, HEAD_DIM, 1).astype(BF16)
    t = _dot(xb, w_ref[:, va_off:va_off + SWA_KV_WIDTH])
    va_ref[...] = t.astype(BF16)
    vas_ref[...] = pltpu.roll(t, HEAD_DIM, 1).astype(BF16)
    qb_ref[...] = (_dot(xb, w_ref[:, qb_off:qb_off + SB_WIDTH]) * (scale * LOG2E)).astype(BF16)
    kb_ref[...] = _dot(xb, w_ref[:, kb_off:kb_off + SB_WIDTH]).astype(BF16)
    vb_ref[...] = _dot(xb, w_ref[:, vb_off:vb_off + SB_WIDTH]).astype(BF16)
    gw = 512
    for c in range(GATE_WIDTH // gw):
        gl = (_dot(xb, w_ref[:, g_off + c * gw:g_off + (c + 1) * gw])
              + bg_ref[:, c * gw:(c + 1) * gw])
        g_ref[:, c * gw:(c + 1) * gw] = jax.nn.sigmoid(gl).astype(BF16)


def _proj_call(x2, pos2, invf, sign, w_in, b_gate):
    m = x2.shape[0]
    tm = PROJ_TM
    row = lambda w: pl.BlockSpec((tm, w), lambda i: (i, 0))
    full = _resident_spec
    out_widths = [SWA_Q_WIDTH, SWA_KV_WIDTH, SWA_KV_WIDTH, SWA_KV_WIDTH, SWA_KV_WIDTH,
                  SB_WIDTH, SB_WIDTH, SB_WIDTH, GATE_WIDTH]
    return pl.pallas_call(
        _proj_kernel,
        grid=(m // tm,),
        in_specs=[row(D_MODEL), pl.BlockSpec((tm // POS_GROUPS, LANES), lambda i: (i, 0)),
                  full(invf), full(sign), full(w_in), full(b_gate)],
        out_specs=[row(w) for w in out_widths],
        out_shape=[jax.ShapeDtypeStruct((m, w), BF16) for w in out_widths],
        compiler_params=pltpu.CompilerParams(
            dimension_semantics=("arbitrary",), vmem_limit_bytes=VMEM_LIMIT),
        name="proj",
    )(x2, pos2, invf, sign, w_in, b_gate)


def _swa_kernel(sinks_ref, q_ref, k_ref, ks_ref, v_ref, vs_ref, o_ref):
    blk = WINDOW
    lane = lax.broadcasted_iota(jnp.int32, (1, LANES), 1)
    low = lane < HEAD_DIM
    group = SWA_Q_HEADS // SWA_KV_HEADS
    heads = range(SWA_Q_HEADS)
    plain = [(h // group) == (h % 2) for h in heads]
    row_minus_col = (lax.broadcasted_iota(jnp.int32, (blk, 2 * blk), 0)
                     - lax.broadcasted_iota(jnp.int32, (blk, 2 * blk), 1))

    sinks2 = [sinks_ref[h] * LOG2E for h in heads]

    @pl.loop(0, q_ref.shape[0] // (SWA_QBLOCKS * blk))
    def _(i):
        tiles = []
        for u in range(SWA_QBLOCKS):
            qstart = pl.multiple_of((i * SWA_QBLOCKS + u) * blk, blk)
            kstart = pl.multiple_of(jnp.maximum(qstart - blk, 0), blk)
            rel = row_minus_col + (qstart - kstart)
            mask = (rel >= 0) & (rel < WINDOW)
            ksl = pl.ds(kstart, 2 * blk)
            tiles.append((pl.ds(qstart, blk), mask, (k_ref[ksl, :], ks_ref[ksl, :]),
                          (v_ref[ksl, :], vs_ref[ksl, :])))

        work = [(t, h) for t in tiles for h in heads]
        qms = []
        for (qrows, _, _, _), h in work:
            qc = q_ref[qrows, (h // 2) * LANES:(h // 2 + 1) * LANES]
            qms.append(jnp.where(low if h % 2 == 0 else ~low, qc, jnp.zeros_like(qc)))
        ss = [jnp.where(t[1], _dot_nt(qm, t[2][0] if plain[h] else t[2][1]), -1e30)
              for (t, h), qm in zip(work, qms)]
        mxs = [jnp.maximum(jnp.max(s, axis=-1, keepdims=True), sinks2[h])
               for (_, h), s in zip(work, ss)]
        ps = [jnp.exp2(s - mx) for s, mx in zip(ss, mxs)]
        denoms = [jnp.sum(p, axis=-1, keepdims=True) + jnp.exp2(sinks2[h] - mx)
                  for (_, h), p, mx in zip(work, ps, mxs)]
        pvs = [_dot(p.astype(BF16), t[3][0] if plain[h] else t[3][1])
               for (t, h), p in zip(work, ps)]
        outs = [pv / d for pv, d in zip(pvs, denoms)]
        for n in range(0, len(work), 2):
            (qrows, _, _, _), h = work[n]
            o_ref[qrows, (h // 2) * LANES:(h // 2 + 1) * LANES] = jnp.where(
                low, outs[n], outs[n + 1]).astype(BF16)


def _swa_call(sinks, qa, ka, kas, va, vas, batch, seq):
    m = qa.shape[0]
    qspec = pl.BlockSpec((seq, SWA_Q_WIDTH), lambda b: (b, 0))
    kvspec = pl.BlockSpec((seq, SWA_KV_WIDTH), lambda b: (b, 0))
    return pl.pallas_call(
        _swa_kernel,
        grid=(batch,),
        in_specs=[pl.BlockSpec(memory_space=pltpu.SMEM), qspec, kvspec, kvspec, kvspec, kvspec],
        out_specs=qspec,
        out_shape=jax.ShapeDtypeStruct((m, SWA_Q_WIDTH), BF16),
        compiler_params=pltpu.CompilerParams(
            dimension_semantics=("arbitrary",), vmem_limit_bytes=VMEM_LIMIT),
        name="swa",
    )(sinks, qa, ka, kas, va, vas)


def _softplus2(z):
    neg_abs = lax.bitcast_convert_type(
        lax.bitcast_convert_type(z, jnp.uint32) | jnp.uint32(0x80000000), F32)
    return jnp.maximum(z, 0.0) + jnp.log2(1.0 + jnp.exp2(neg_abs))


def _sb_kernel(q_ref, k_ref, v_ref, o_ref, acc_ref, carry_ref, factor_ref):
    tq, tk = SB_TQ, SB_TK
    chunks = q_ref.shape[1] // LANES
    lane = lax.broadcasted_iota(jnp.int32, (1, LANES), 1)
    low = lane < HEAD_DIM
    rr = lax.broadcasted_iota(jnp.int32, (tk, tk), 0)
    cc = lax.broadcasted_iota(jnp.int32, (tk, tk), 1)
    incl_tri = jnp.where(rr >= cc, 1.0, 0.0).astype(BF16)
    heads = range(2 * chunks)
    half = tq // 2

    def tile(row0, j, nrows, diag):
        qms = []
        for c in range(chunks):
            q = q_ref[pl.ds(row0, nrows), c * LANES:(c + 1) * LANES]
            qms.append(jnp.where(low, q, jnp.zeros_like(q)))
            qms.append(jnp.where(low, jnp.zeros_like(q), q))
        start = pl.multiple_of(j * tk, tk)
        kts = [k_ref[pl.ds(start, tk), c * LANES:(c + 1) * LANES] for c in range(chunks)]
        vts = [v_ref[pl.ds(start, tk), c * LANES:(c + 1) * LANES] for c in range(chunks)]
        if diag:
            causal = (lax.broadcasted_iota(jnp.int32, (nrows, tk), 1)
                      < lax.broadcasted_iota(jnp.int32, (nrows, tk), 0))
        zs = [_dot_nt(qms[h], kts[h // 2]) for h in heads]
        sps = [_softplus2(z) for z in zs]
        if diag:
            sps = [jnp.where(causal, sp, 0.0) for sp in sps]
        incls = [_dot(sp.astype(BF16), incl_tri) for sp in sps]
        aa = [jnp.exp2(z - incl) for z, incl in zip(zs, incls)]
        if diag:
            aa = [jnp.where(causal, a, 0.0) for a in aa]
        pvs = [_dot(aa[h].astype(BF16), vts[h // 2]) for h in heads]
        sums = [jnp.broadcast_to(incl[:, 0:1], (nrows, LANES)) for incl in incls]
        pv = [jnp.where(low, pvs[2 * c], pvs[2 * c + 1]) for c in range(chunks)]
        mass = [jnp.where(low, sums[2 * c], sums[2 * c + 1]) for c in range(chunks)]
        return pv, mass

    def advance(mass, nrows, first):
        fmax = None
        for c in range(chunks):
            sl = slice(c * LANES, (c + 1) * LANES)
            carry = mass[c] if first else carry_ref[0:nrows, sl] + mass[c]
            carry_ref[0:nrows, sl] = carry
            f = jnp.exp2(-carry)
            factor_ref[0:nrows, sl] = f
            fmax = f if fmax is None else jnp.maximum(fmax, f)
        alive_lo = (jnp.max(fmax[0:half]) > 0.0).astype(jnp.int32)
        if nrows == half:
            return alive_lo
        return alive_lo, (jnp.max(fmax[half:]) > 0.0).astype(jnp.int32)

    def update(row0, j, nrows):
        pv, mass = tile(row0, j, nrows, False)
        for c in range(chunks):
            sl = slice(c * LANES, (c + 1) * LANES)
            acc_ref[0:nrows, sl] += factor_ref[0:nrows, sl] * pv[c]
        return advance(mass, nrows, False)

    @pl.loop(0, q_ref.shape[0] // tq)
    def _(i):
        row0 = pl.multiple_of(i * tq, tq)
        pv, mass = tile(row0, i, tq, True)
        for c in range(chunks):
            acc_ref[:, c * LANES:(c + 1) * LANES] = pv[c]
        alive_lo, alive_hi = advance(mass, tq, True)

        def cond(state):
            n, alive_lo, alive_hi = state
            return (n < i) & ((alive_lo | alive_hi) > 0)

        def body(state):
            n, _, alive_hi = state
            j = i - 1 - n
            alive_lo, alive_hi = lax.cond(
                alive_hi > 0,
                lambda: update(row0, j, tq),
                lambda: (update(row0, j, half), jnp.int32(0)))
            return n + 1, alive_lo, alive_hi

        lax.while_loop(cond, body, (jnp.int32(0), alive_lo, alive_hi))
        o_ref[pl.ds(row0, tq), :] = acc_ref[...].astype(BF16)


def _sb_call(qb, kb, vb, batch, seq):
    m = qb.shape[0]
    width = SB_CHUNKS * LANES
    groups = SB_WIDTH // width
    spec = pl.BlockSpec((seq, width), lambda b, p: (b, p))
    return pl.pallas_call(
        _sb_kernel,
        grid=(batch, groups),
        in_specs=[spec, spec, spec],
        out_specs=spec,
        out_shape=jax.ShapeDtypeStruct((m, SB_WIDTH), BF16),
        scratch_shapes=[pltpu.VMEM((SB_TQ, width), F32)] * 3,
        compiler_params=pltpu.CompilerParams(
            dimension_semantics=("arbitrary", "arbitrary"),
            vmem_limit_bytes=VMEM_LIMIT),
        name="stickbreak",
    )(qb, kb, vb)


def _merge_kernel(alpha, x_ref, ya_ref, yb_ref, g_ref, wa_ref, wb_ref, wo_ref, lg_ref, lb_ref, o_ref):
    sub = MERGE_SUB
    rows = [slice(r, r + sub) for r in range(0, MERGE_TM, sub)]
    pas = [_dot(ya_ref[r, :], wa_ref[...]) for r in rows]
    pbs = [_dot(yb_ref[r, :], wb_ref[...]) for r in rows]
    hs = [(g_ref[r, 0:D_MODEL].astype(F32) * pa
           + g_ref[r, D_MODEL:2 * D_MODEL].astype(F32) * pb).astype(BF16)
          for r, pa, pb in zip(rows, pas, pbs)]
    res = [alpha * x_ref[r, :] + _dot(h, wo_ref[...]) for r, h in zip(rows, hs)]
    for r, v in zip(rows, res):
        o_ref[r, :] = _layer_norm(v, lg_ref[...], lb_ref[...])


def _merge_call(alpha, x2, ya, yb, gates, wa, wb, wo, lg, lb):
    m = x2.shape[0]
    tm = MERGE_TM
    row = lambda w: pl.BlockSpec((tm, w), lambda i: (i, 0))
    full = _resident_spec
    return pl.pallas_call(
        functools.partial(_merge_kernel, alpha),
        grid=(m // tm,),
        in_specs=[row(D_MODEL), row(SWA_Q_WIDTH), row(SB_WIDTH), row(GATE_WIDTH),
                  full(wa), full(wb), full(wo), full(lg), full(lb)],
        out_specs=row(D_MODEL),
        out_shape=jax.ShapeDtypeStruct((m, D_MODEL), F32),
        compiler_params=pltpu.CompilerParams(
            dimension_semantics=("arbitrary",), vmem_limit_bytes=VMEM_LIMIT),
        name="merge",
    )(x2, ya, yb, gates, wa, wb, wo, lg, lb)


def _ffn_kernel(alpha, tiles_per_seq, x_ref, halo_ref, wu_ref, cw_ref, cb_ref, wd_ref,
                lg_ref, lb_ref, o_ref, act_ref):
    i = pl.program_id(0)
    tm, hl, cwid = FFN_TM, FFN_HALO, FFN_CW
    x = x_ref[...]
    seq_start = (i % tiles_per_seq) == 0
    halo = jnp.where(seq_start, 0.0, halo_ref[...])
    xe = jnp.concatenate([halo.astype(BF16), x.astype(BF16)], axis=0)

    def conv(u, col):
        sl = slice(col, col + cwid)
        y = (cw_ref[2:3, sl] * u[hl:, :]
             + cw_ref[1:2, sl] * pltpu.roll(u, 1, 0)[hl:, :]
             + cw_ref[0:1, sl] * pltpu.roll(u, 2, 0)[hl:, :])
        return y + cb_ref[:, sl]

    def up_dots(c):
        gcol = c * cwid
        ucol = D_FF + c * cwid
        return _dot(xe, wu_ref[:, gcol:gcol + cwid]), _dot(xe, wu_ref[:, ucol:ucol + cwid])

    nchunks = D_FF // cwid
    ahead = FFN_AHEAD
    pending = [up_dots(c) for c in range(ahead)]
    for c in range(nchunks):
        ug, uu = pending.pop(0)
        if c + ahead < nchunks:
            pending.append(up_dots(c + ahead))
        gcol = c * cwid
        gate = conv(ug, gcol)
        up = conv(uu, D_FF + gcol)
        act_ref[:, gcol:gcol + cwid] = (gate * jax.nn.sigmoid(gate) * up).astype(BF16)
    for r in range(0, tm, FFN_DOWN_ROWS):
        rows = slice(r, r + FFN_DOWN_ROWS)
        f = _dot(act_ref[rows, :], wd_ref[...])
        o_ref[rows, :] = _layer_norm(alpha * x_ref[rows, :] + f, lg_ref[...], lb_ref[...])


def _ffn_call(alpha, x1, wu, cw, cb, wd, lg, lb, seq):
    m = x1.shape[0]
    tm, hl = FFN_TM, FFN_HALO
    full = _resident_spec
    halo_blocks = tm // hl
    return pl.pallas_call(
        functools.partial(_ffn_kernel, alpha, seq // tm),
        grid=(m // tm,),
        in_specs=[pl.BlockSpec((tm, D_MODEL), lambda i: (i, 0)),
                  pl.BlockSpec((hl, D_MODEL), lambda i: (jnp.maximum(i * halo_blocks - 1, 0), 0)),
                  full(wu), full(cw), full(cb), full(wd), full(lg), full(lb)],
        out_specs=pl.BlockSpec((tm, D_MODEL), lambda i: (i, 0)),
        out_shape=jax.ShapeDtypeStruct((m, D_MODEL), F32),
        scratch_shapes=[pltpu.VMEM((tm, D_FF), BF16)],
        compiler_params=pltpu.CompilerParams(
            dimension_semantics=("arbitrary",), vmem_limit_bytes=VMEM_LIMIT),
        name="convffn",
    )(x1, x1, wu, cw, cb, wd, lg, lb)


def kernel(x, positions, w_in, b_gate, sinks, w_branch_a, w_branch_b, w_out, ln1_g, ln1_b,
           w_up, conv_w, conv_b, w_down, ln2_g, ln2_b):
    batch, seq, _ = x.shape
    depth = w_in.shape[0]
    alpha = (2.0 * depth) ** 0.25
    m = batch * seq
    assert seq % SB_TQ == 0 and seq % FFN_TM == 0 and seq % WINDOW == 0
    assert m % PROJ_TM == 0 and m % MERGE_TM == 0

    half = jnp.arange(0, HEAD_DIM, 2, dtype=F32) / HEAD_DIM
    inv_freq = 1.0 / (ROPE_THETA ** half)
    invf = jnp.tile(inv_freq, LANES // (HEAD_DIM // 2))[None, :]
    sign = jnp.tile(jnp.concatenate([-jnp.ones(HEAD_DIM // 2, F32), jnp.ones(HEAD_DIM // 2, F32)]),
                    LANES // HEAD_DIM)[None, :]
    slab = PROJ_TM // POS_GROUPS
    pos2 = positions.reshape(m // PROJ_TM, POS_GROUPS, slab).transpose(0, 2, 1)
    pos2 = jnp.repeat(pos2, LANES // POS_GROUPS, axis=2).reshape(m // POS_GROUPS, LANES)
    x2 = x.reshape(m, D_MODEL)

    for l in range(depth):
        qa, ka, kas, va, vas, qb, kb, vb, gates = _proj_call(
            x2, pos2, invf, sign, w_in[l].astype(BF16), b_gate[l][None, :])
        ya = _swa_call(sinks[l], qa, ka, kas, va, vas, batch, seq)
        yb = _sb_call(qb, kb, vb, batch, seq)
        x1 = _merge_call(alpha, x2, ya, yb, gates,
                         w_branch_a[l].astype(BF16), w_branch_b[l].astype(BF16),
                         w_out[l].astype(BF16), ln1_g[l][None, :], ln1_b[l][None, :])
        x2 = _ffn_call(alpha, x1, w_up[l].astype(BF16), conv_w[l], conv_b[l][None, :],
                       w_down[l].astype(BF16), ln2_g[l][None, :], ln2_b[l][None, :], seq)
    return x2.reshape(batch, seq, D_MODEL)
```

```python
import functools
import math

import jax
import jax.numpy as jnp
from jax import lax
from jax.experimental import pallas as pl
from jax.experimental.pallas import tpu as pltpu

D_MODEL = 1024
HEAD_DIM = 64
SWA_Q_HEADS = 8
SWA_KV_HEADS = 2
SB_HEADS = 8
WINDOW = 128
ROPE_THETA = 10000.0
D_FF = 2816
LN_EPS = 1e-5

SWA_Q_WIDTH = SWA_Q_HEADS * HEAD_DIM
SWA_KV_WIDTH = SWA_KV_HEADS * HEAD_DIM
SB_WIDTH = SB_HEADS * HEAD_DIM
GATE_WIDTH = 2 * D_MODEL

LANES = 128
VMEM_LIMIT = 56 * 1024 * 1024

PROJ_TM = 1024
POS_GROUPS = LANES // (HEAD_DIM // 2)
MERGE_TM = 1024
MERGE_SUB = 128
FFN_TM = 1024
FFN_CW = 256
FFN_AHEAD = 2
FFN_DOWN_ROWS = 256
FFN_HALO = 16
SWA_QBLOCKS = 2
SB_TQ = 256
SB_TK = 256

LOG2E = math.log2(math.e)

F32 = jnp.float32
BF16 = jnp.bfloat16


def _dot(a, b):
    return jnp.dot(a, b, preferred_element_type=F32)


def _dot_nt(a, b):
    return lax.dot_general(a, b, (((1,), (1,)), ((), ())), preferred_element_type=F32)


def _resident_spec(a):
    return pl.BlockSpec(a.shape, lambda i: (0, 0), pipeline_mode=pl.Buffered(1))


def _layer_norm(v, g, b):
    mu = jnp.mean(v, axis=-1, keepdims=True)
    vc = v - mu
    var = jnp.mean(vc * vc, axis=-1, keepdims=True)
    return vc * lax.rsqrt(var + LN_EPS) * g + b


def _proj_kernel(x_ref, pos_ref, invf_ref, sign_ref, w_ref, bg_ref,
                 qa_ref, ka_ref, kas_ref, va_ref, vas_ref, qb_ref, kb_ref, vb_ref, g_ref):
    xb = x_ref[...].astype(BF16)
    lane = lax.broadcasted_iota(jnp.int32, (1, LANES), 1)
    first_half = (lane % HEAD_DIM) < (HEAD_DIM // 2)
    nfreq = HEAD_DIM // 2
    ngroups = POS_GROUPS
    ang = pos_ref[...].astype(F32) * invf_ref[...]
    cos_rolled = [jnp.cos(ang)]
    sin_rolled = [jnp.sin(ang)]
    for s in range(1, ngroups):
        cos_rolled.append(pltpu.roll(cos_rolled[0], nfreq * s, 1))
        sin_rolled.append(pltpu.roll(sin_rolled[0], nfreq * s, 1))
    lane_group = lane // nfreq

    def unpack(rolled, g):
        out = rolled[(0 - g) % ngroups]
        for k in range(1, ngroups):
            out = jnp.where(lane_group == k, rolled[(k - g) % ngroups], out)
        return out

    cosv = jnp.concatenate([unpack(cos_rolled, g) for g in range(ngroups)], axis=0)
    sinv = jnp.concatenate([unpack(sin_rolled, g) for g in range(ngroups)], axis=0) * sign_ref[...]

    def rope(t):
        fwd = pltpu.roll(t, HEAD_DIM // 2, 1)
        bwd = pltpu.roll(t, LANES - HEAD_DIM // 2, 1)
        rot = jnp.where(first_half, bwd, fwd)
        return t * cosv + rot * sinv

    scale = HEAD_DIM ** -0.5
    qa_off = 0
    ka_off = qa_off + SWA_Q_WIDTH
    va_off = ka_off + SWA_KV_WIDTH
    qb_off = va_off + SWA_KV_WIDTH
    kb_off = qb_off + SB_WIDTH
    vb_off = kb_off + SB_WIDTH
    g_off = vb_off + SB_WIDTH
    t = _dot(xb, w_ref[:, qa_off:qa_off + SWA_Q_WIDTH])
    for c in range(SWA_Q_WIDTH // LANES):
        qa_ref[:, c * LANES:(c + 1) * LANES] = (
            rope(t[:, c * LANES:(c + 1) * LANES]) * (scale * LOG2E)).astype(BF16)
    t = rope(_dot(xb, w_ref[:, ka_off:ka_off + SWA_KV_WIDTH]))
    ka_ref[...] = t.astype(BF16)
    kas_ref[...] = pltpu.roll(t, HEAD_DIM, 1).astype(BF16)
    t = _dot(xb, w_ref[:, va_off:va_off + SWA_KV_WIDTH])
    va_ref[...] = t.astype(BF16)
    vas_ref[...] = pltpu.roll(t, HEAD_DIM, 1).astype(BF16)
    qb_ref[...] = (_dot(xb, w_ref[:, qb_off:qb_off + SB_WIDTH]) * (scale * LOG2E)).astype(BF16)
    kb_ref[...] = _dot(xb, w_ref[:, kb_off:kb_off + SB_WIDTH]).astype(BF16)
    vb_ref[...] = _dot(xb, w_ref[:, vb_off:vb_off + SB_WIDTH]).astype(BF16)
    gw = 512
    for c in range(GATE_WIDTH // gw):
        gl = (_dot(xb, w_ref[:, g_off + c * gw:g_off + (c + 1) * gw])
              + bg_ref[:, c * gw:(c + 1) * gw])
        g_ref[:, c * gw:(c + 1) * gw] = jax.nn.sigmoid(gl).astype(BF16)


def _proj_call(x2, pos2, invf, sign, w_in, b_gate):
    m = x2.shape[0]
    tm = PROJ_TM
    row = lambda w: pl.BlockSpec((tm, w), lambda i: (i, 0))
    full = _resident_spec
    out_widths = [SWA_Q_WIDTH, SWA_KV_WIDTH, SWA_KV_WIDTH, SWA_KV_WIDTH, SWA_KV_WIDTH,
                  SB_WIDTH, SB_WIDTH, SB_WIDTH, GATE_WIDTH]
    return pl.pallas_call(
        _proj_kernel,
        grid=(m // tm,),
        in_specs=[row(D_MODEL), pl.BlockSpec((tm // POS_GROUPS, LANES), lambda i: (i, 0)),
                  full(invf), full(sign), full(w_in), full(b_gate)],
        out_specs=[row(w) for w in out_widths],
        out_shape=[jax.ShapeDtypeStruct((m, w), BF16) for w in out_widths],
        compiler_params=pltpu.CompilerParams(
            dimension_semantics=("arbitrary",), vmem_limit_bytes=VMEM_LIMIT),
        name="proj",
    )(x2, pos2, invf, sign, w_in, b_gate)


def _swa_kernel(sinks_ref, q_ref, k_ref, ks_ref, v_ref, vs_ref, o_ref):
    blk = WINDOW
    lane = lax.broadcasted_iota(jnp.int32, (1, LANES), 1)
    low = lane < HEAD_DIM
    group = SWA_Q_HEADS // SWA_KV_HEADS
    heads = range(SWA_Q_HEADS)
    plain = [(h // group) == (h % 2) for h in heads]
    row_minus_col = (lax.broadcasted_iota(jnp.int32, (blk, 2 * blk), 0)
                     - lax.broadcasted_iota(jnp.int32, (blk, 2 * blk), 1))

    sinks2 = [sinks_ref[h] * LOG2E for h in heads]

    @pl.loop(0, q_ref.shape[0] // (SWA_QBLOCKS * blk))
    def _(i):
        tiles = []
        for u in range(SWA_QBLOCKS):
            qstart = pl.multiple_of((i * SWA_QBLOCKS + u) * blk, blk)
            kstart = pl.multiple_of(jnp.maximum(qstart - blk, 0), blk)
            rel = row_minus_col + (qstart - kstart)
            mask = (rel >= 0) & (rel < WINDOW)
            ksl = pl.ds(kstart, 2 * blk)
            tiles.append((pl.ds(qstart, blk), mask, (k_ref[ksl, :], ks_ref[ksl, :]),
                          (v_ref[ksl, :], vs_ref[ksl, :])))

        work = [(t, h) for t in tiles for h in heads]
        qms = []
        for (qrows, _, _, _), h in work:
            qc = q_ref[qrows, (h // 2) * LANES:(h // 2 + 1) * LANES]
            qms.append(jnp.where(low if h % 2 == 0 else ~low, qc, jnp.zeros_like(qc)))
        ss = [jnp.where(t[1], _dot_nt(qm, t[2][0] if plain[h] else t[2][1]), -1e30)
              for (t, h), qm in zip(work, qms)]
        mxs = [jnp.maximum(jnp.max(s, axis=-1, keepdims=True), sinks2[h])
               for (_, h), s in zip(work, ss)]
        ps = [jnp.exp2(s - mx) for s, mx in zip(ss, mxs)]
        denoms = [jnp.sum(p, axis=-1, keepdims=True) + jnp.exp2(sinks2[h] - mx)
                  for (_, h), p, mx in zip(work, ps, mxs)]
        pvs = [_dot(p.astype(BF16), t[3][0] if plain[h] else t[3][1])
               for (t, h), p in zip(work, ps)]
        outs = [pv / d for pv, d in zip(pvs, denoms)]
        for n in range(0, len(work), 2):
            (qrows, _, _, _), h = work[n]
            o_ref[qrows, (h // 2) * LANES:(h // 2 + 1) * LANES] = jnp.where(
                low, outs[n], outs[n + 1]).astype(BF16)


def _swa_call(sinks, qa, ka, kas, va, vas, batch, seq):
    m = qa.shape[0]
    qspec = pl.BlockSpec((seq, SWA_Q_WIDTH), lambda b: (b, 0))
    kvspec = pl.BlockSpec((seq, SWA_KV_WIDTH), lambda b: (b, 0))
    return pl.pallas_call(
        _swa_kernel,
        grid=(batch,),
        in_specs=[pl.BlockSpec(memory_space=pltpu.SMEM), qspec, kvspec, kvspec, kvspec, kvspec],
        out_specs=qspec,
        out_shape=jax.ShapeDtypeStruct((m, SWA_Q_WIDTH), BF16),
        compiler_params=pltpu.CompilerParams(
            dimension_semantics=("arbitrary",), vmem_limit_bytes=VMEM_LIMIT),
        name="swa",
    )(sinks, qa, ka, kas, va, vas)


def _softplus2(z):
    neg_abs = lax.bitcast_convert_type(
        lax.bitcast_convert_type(z, jnp.uint32) | jnp.uint32(0x80000000), F32)
    return jnp.maximum(z, 0.0) + jnp.log2(1.0 + jnp.exp2(neg_abs))


def _sb_kernel(q_ref, k_ref, v_ref, o_ref, acc_ref, carry_ref, factor_ref):
    tq, tk = SB_TQ, SB_TK
    chunks = q_ref.shape[1] // LANES
    lane = lax.broadcasted_iota(jnp.int32, (1, LANES), 1)
    low = lane < HEAD_DIM
    rr = lax.broadcasted_iota(jnp.int32, (tk, tk), 0)
    cc = lax.broadcasted_iota(jnp.int32, (tk, tk), 1)
    incl_tri = jnp.where(rr >= cc, 1.0, 0.0).astype(BF16)
    heads = range(2 * chunks)
    half = tq // 2

    def tile(row0, j, nrows, diag):
        qms = []
        for c in range(chunks):
            q = q_ref[pl.ds(row0, nrows), c * LANES:(c + 1) * LANES]
            qms.append(jnp.where(low, q, jnp.zeros_like(q)))
            qms.append(jnp.where(low, jnp.zeros_like(q), q))
        start = pl.multiple_of(j * tk, tk)
        kts = [k_ref[pl.ds(start, tk), c * LANES:(c + 1) * LANES] for c in range(chunks)]
        vts = [v_ref[pl.ds(start, tk), c * LANES:(c + 1) * LANES] for c in range(chunks)]
        if diag:
            causal = (lax.broadcasted_iota(jnp.int32, (nrows, tk), 1)
                      < lax.broadcasted_iota(jnp.int32, (nrows, tk), 0))
        zs = [_dot_nt(qms[h], kts[h // 2]) for h in heads]
        sps = [_softplus2(z) for z in zs]
        if diag:
            sps = [jnp.where(causal, sp, 0.0) for sp in sps]
        incls = [_dot(sp.astype(BF16), incl_tri) for sp in sps]
        aa = [jnp.exp2(z - incl) for z, incl in zip(zs, incls)]
        if diag:
            aa = [jnp.where(causal, a, 0.0) for a in aa]
        pvs = [_dot(aa[h].astype(BF16), vts[h // 2]) for h in heads]
        sums = [jnp.broadcast_to(incl[:, 0:1], (nrows, LANES)) for incl in incls]
        pv = [jnp.where(low, pvs[2 * c], pvs[2 * c + 1]) for c in range(chunks)]
        mass = [jnp.where(low, sums[2 * c], sums[2 * c + 1]) for c in range(chunks)]
        return pv, mass

    def advance(mass, nrows, first):
        fmax = None
        for c in range(chunks):
            sl = slice(c * LANES, (c + 1) * LANES)
            carry = mass[c] if first else carry_ref[0:nrows, sl] + mass[c]
            carry_ref[0:nrows, sl] = carry
            f = jnp.exp2(-carry)
            factor_ref[0:nrows, sl] = f
            fmax = f if fmax is None else jnp.maximum(fmax, f)
        alive_lo = (jnp.max(fmax[0:half]) > 0.0).astype(jnp.int32)
        if nrows == half:
            return alive_lo
        return alive_lo, (jnp.max(fmax[half:]) > 0.0).astype(jnp.int32)

    def update(row0, j, nrows):
        pv, mass = tile(row0, j, nrows, False)
        for c in range(chunks):
            sl = slice(c * LANES, (c + 1) * LANES)
            acc_ref[0:nrows, sl] += factor_ref[0:nrows, sl] * pv[c]
        return advance(mass, nrows, False)

    @pl.loop(0, q_ref.shape[0] // tq)
    def _(i):
        row0 = pl.multiple_of(i * tq, tq)
        pv, mass = tile(row0, i, tq, True)
        for c in range(chunks):
            acc_ref[:, c * LANES:(c + 1) * LANES] = pv[c]
        alive_lo, alive_hi = advance(mass, tq, True)

        def cond(state):
            n, alive_lo, alive_hi = state
            return (n < i) & ((alive_lo | alive_hi) > 0)

        def body(state):
            n, _, alive_hi = state
            j = i - 1 - n
            alive_lo, alive_hi = lax.cond(
                alive_hi > 0,
                lambda: update(row0, j, tq),
                lambda: (update(row0, j, half), jnp.int32(0)))
            return n + 1, alive_lo, alive_hi

        lax.while_loop(cond, body, (jnp.int32(0), alive_lo, alive_hi))
        o_ref[pl.ds(row0, tq), :] = acc_ref[...].astype(BF16)


def _sb_call(qb, kb, vb, batch, seq):
    m = qb.shape[0]
    spec = pl.BlockSpec((seq, SB_WIDTH), lambda b: (b, 0))
    return pl.pallas_call(
        _sb_kernel,
        grid=(batch,),
        in_specs=[spec, spec, spec],
        out_specs=spec,
        out_shape=jax.ShapeDtypeStruct((m, SB_WIDTH), BF16),
        scratch_shapes=[pltpu.VMEM((SB_TQ, SB_WIDTH), F32)] * 3,
        compiler_params=pltpu.CompilerParams(
            dimension_semantics=("arbitrary",), vmem_limit_bytes=VMEM_LIMIT),
        name="stickbreak",
    )(qb, kb, vb)


def _merge_kernel(alpha, x_ref, ya_ref, yb_ref, g_ref, wa_ref, wb_ref, wo_ref, lg_ref, lb_ref, o_ref):
    sub = MERGE_SUB
    rows = [slice(r, r + sub) for r in range(0, MERGE_TM, sub)]
    pas = [_dot(ya_ref[r, :], wa_ref[...]) for r in rows]
    pbs = [_dot(yb_ref[r, :], wb_ref[...]) for r in rows]
    hs = [(g_ref[r, 0:D_MODEL].astype(F32) * pa
           + g_ref[r, D_MODEL:2 * D_MODEL].astype(F32) * pb).astype(BF16)
          for r, pa, pb in zip(rows, pas, pbs)]
    res = [alpha * x_ref[r, :] + _dot(h, wo_ref[...]) for r, h in zip(rows, hs)]
    for r, v in zip(rows, res):
        o_ref[r, :] = _layer_norm(v, lg_ref[...], lb_ref[...])


def _merge_call(alpha, x2, ya, yb, gates, wa, wb, wo, lg, lb):
    m = x2.shape[0]
    tm = MERGE_TM
    row = lambda w: pl.BlockSpec((tm, w), lambda i: (i, 0))
    full = _resident_spec
    return pl.pallas_call(
        functools.partial(_merge_kernel, alpha),
        grid=(m // tm,),
        in_specs=[row(D_MODEL), row(SWA_Q_WIDTH), row(SB_WIDTH), row(GATE_WIDTH),
                  full(wa), full(wb), full(wo), full(lg), full(lb)],
        out_specs=row(D_MODEL),
        out_shape=jax.ShapeDtypeStruct((m, D_MODEL), F32),
        compiler_params=pltpu.CompilerParams(
            dimension_semantics=("arbitrary",), vmem_limit_bytes=VMEM_LIMIT),
        name="merge",
    )(x2, ya, yb, gates, wa, wb, wo, lg, lb)


def _ffn_kernel(alpha, tiles_per_seq, x_ref, halo_ref, wu_ref, cw_ref, cb_ref, wd_ref,
                lg_ref, lb_ref, o_ref, act_ref):
    i = pl.program_id(0)
    tm, hl, cwid = FFN_TM, FFN_HALO, FFN_CW
    x = x_ref[...]
    seq_start = (i % tiles_per_seq) == 0
    halo = jnp.where(seq_start, 0.0, halo_ref[...])
    xe = jnp.concatenate([halo.astype(BF16), x.astype(BF16)], axis=0)

    def conv(u, col):
        sl = slice(col, col + cwid)
        y = (cw_ref[2:3, sl] * u[hl:, :]
             + cw_ref[1:2, sl] * pltpu.roll(u, 1, 0)[hl:, :]
             + cw_ref[0:1, sl] * pltpu.roll(u, 2, 0)[hl:, :])
        return y + cb_ref[:, sl]

    def up_dots(c):
        gcol = c * cwid
        ucol = D_FF + c * cwid
        return _dot(xe, wu_ref[:, gcol:gcol + cwid]), _dot(xe, wu_ref[:, ucol:ucol + cwid])

    nchunks = D_FF // cwid
    ahead = FFN_AHEAD
    pending = [up_dots(c) for c in range(ahead)]
    for c in range(nchunks):
        ug, uu = pending.pop(0)
        if c + ahead < nchunks:
            pending.append(up_dots(c + ahead))
        gcol = c * cwid
        gate = conv(ug, gcol)
        up = conv(uu, D_FF + gcol)
        act_ref[:, gcol:gcol + cwid] = (gate * jax.nn.sigmoid(gate) * up).astype(BF16)
    for r in range(0, tm, FFN_DOWN_ROWS):
        rows = slice(r, r + FFN_DOWN_ROWS)
        f = _dot(act_ref[rows, :], wd_ref[...])
        o_ref[rows, :] = _layer_norm(alpha * x_ref[rows, :] + f, lg_ref[...], lb_ref[...])


def _ffn_call(alpha, x1, wu, cw, cb, wd, lg, lb, seq):
    m = x1.shape[0]
    tm, hl = FFN_TM, FFN_HALO
    full = _resident_spec
    halo_blocks = tm // hl
    return pl.pallas_call(
        functools.partial(_ffn_kernel, alpha, seq // tm),
        grid=(m // tm,),
        in_specs=[pl.BlockSpec((tm, D_MODEL), lambda i: (i, 0)),
                  pl.BlockSpec((hl, D_MODEL), lambda i: (jnp.maximum(i * halo_blocks - 1, 0), 0)),
                  full(wu), full(cw), full(cb), full(wd), full(lg), full(lb)],
        out_specs=pl.BlockSpec((tm, D_MODEL), lambda i: (i, 0)),
        out_shape=jax.ShapeDtypeStruct((m, D_MODEL), F32),
        scratch_shapes=[pltpu.VMEM((tm, D_FF), BF16)],
        compiler_params=pltpu.CompilerParams(
            dimension_semantics=("arbitrary",), vmem_limit_bytes=VMEM_LIMIT),
        name="convffn",
    )(x1, x1, wu, cw, cb, wd, lg, lb)


def kernel(x, positions, w_in, b_gate, sinks, w_branch_a, w_branch_b, w_out, ln1_g, ln1_b,
           w_up, conv_w, conv_b, w_down, ln2_g, ln2_b):
    batch, seq, _ = x.shape
    depth = w_in.shape[0]
    alpha = (2.0 * depth) ** 0.25
    m = batch * seq
    assert seq % SB_TQ == 0 and seq % FFN_TM == 0 and seq % (SWA_QBLOCKS * WINDOW) == 0
    assert m % PROJ_TM == 0 and m % MERGE_TM == 0

    half = jnp.arange(0, HEAD_DIM, 2, dtype=F32) / HEAD_DIM
    inv_freq = 1.0 / (ROPE_THETA ** half)
    invf = jnp.tile(inv_freq, LANES // (HEAD_DIM // 2))[None, :]
    sign = jnp.tile(jnp.concatenate([-jnp.ones(HEAD_DIM // 2, F32), jnp.ones(HEAD_DIM // 2, F32)]),
                    LANES // HEAD_DIM)[None, :]
    slab = PROJ_TM // POS_GROUPS
    pos2 = positions.reshape(m // PROJ_TM, POS_GROUPS, slab).transpose(0, 2, 1)
    pos2 = jnp.repeat(pos2, LANES // POS_GROUPS, axis=2).reshape(m // POS_GROUPS, LANES)
    x2 = x.reshape(m, D_MODEL)

    for l in range(depth):
        qa, ka, kas, va, vas, qb, kb, vb, gates = _proj_call(
            x2, pos2, invf, sign, w_in[l].astype(BF16), b_gate[l][None, :])
        ya = _swa_call(sinks[l], qa, ka, kas, va, vas, batch, seq)
        yb = _sb_call(qb, kb, vb, batch, seq)
        x1 = _merge_call(alpha, x2, ya, yb, gates,
                         w_branch_a[l].astype(BF16), w_branch_b[l].astype(BF16),
                         w_out[l].astype(BF16), ln1_g[l][None, :], ln1_b[l][None, :])
        x2 = _ffn_call(alpha, x1, w_up[l].astype(BF16), conv_w[l], conv_b[l][None, :],
                       w_down[l].astype(BF16), ln2_g[l][None, :], ln2_b[l][None, :], seq)
    return x2.reshape(batch, seq, D_MODEL)
```

```python
import functools
import math

import jax
import jax.numpy as jnp
from jax import lax
from jax.experimental import pallas as pl
from jax.experimental.pallas import tpu as pltpu

D_MODEL = 1024
HEAD_DIM = 64
SWA_Q_HEADS = 8
SWA_KV_HEADS = 2
SB_HEADS = 8
WINDOW = 128
ROPE_THETA = 10000.0
D_FF = 2816
LN_EPS = 1e-5

SWA_Q_WIDTH = SWA_Q_HEADS * HEAD_DIM
SWA_KV_WIDTH = SWA_KV_HEADS * HEAD_DIM
SB_WIDTH = SB_HEADS * HEAD_DIM
GATE_WIDTH = 2 * D_MODEL

LANES = 128
VMEM_LIMIT = 56 * 1024 * 1024

PROJ_TM = 1024
POS_GROUPS = LANES // (HEAD_DIM // 2)
MERGE_TM = 1024
MERGE_SUB = 128
FFN_TM = 1024
FFN_CW = 256
FFN_AHEAD = 2
FFN_DOWN_ROWS = 256
FFN_HALO = 16
SWA_QBLOCKS = 2
SB_TQ = 256
SB_TK = 256
SB_LOWER_ROWS = 160

LOG2E = math.log2(math.e)

F32 = jnp.float32
BF16 = jnp.bfloat16


def _dot(a, b):
    return jnp.dot(a, b, preferred_element_type=F32)


def _dot_nt(a, b):
    return lax.dot_general(a, b, (((1,), (1,)), ((), ())), preferred_element_type=F32)


def _resident_spec(a):
    return pl.BlockSpec(a.shape, lambda i: (0, 0), pipeline_mode=pl.Buffered(1))


def _layer_norm(v, g, b):
    mu = jnp.mean(v, axis=-1, keepdims=True)
    vc = v - mu
    var = jnp.mean(vc * vc, axis=-1, keepdims=True)
    return vc * lax.rsqrt(var + LN_EPS) * g + b


def _proj_kernel(x_ref, pos_ref, invf_ref, sign_ref, w_ref, bg_ref,
                 qa_ref, ka_ref, kas_ref, va_ref, vas_ref, qb_ref, kb_ref, vb_ref, g_ref):
    xb = x_ref[...].astype(BF16)
    lane = lax.broadcasted_iota(jnp.int32, (1, LANES), 1)
    first_half = (lane % HEAD_DIM) < (HEAD_DIM // 2)
    nfreq = HEAD_DIM // 2
    ngroups = POS_GROUPS
    ang = pos_ref[...].astype(F32) * invf_ref[...]
    cos_rolled = [jnp.cos(ang)]
    sin_rolled = [jnp.sin(ang)]
    for s in range(1, ngroups):
        cos_rolled.append(pltpu.roll(cos_rolled[0], nfreq * s, 1))
        sin_rolled.append(pltpu.roll(sin_rolled[0], nfreq * s, 1))
    lane_group = lane // nfreq

    def unpack(rolled, g):
        out = rolled[(0 - g) % ngroups]
        for k in range(1, ngroups):
            out = jnp.where(lane_group == k, rolled[(k - g) % ngroups], out)
        return out

    cosv = jnp.concatenate([unpack(cos_rolled, g) for g in range(ngroups)], axis=0)
    sinv = jnp.concatenate([unpack(sin_rolled, g) for g in range(ngroups)], axis=0) * sign_ref[...]

    def rope(t):
        fwd = pltpu.roll(t, HEAD_DIM // 2, 1)
        bwd = pltpu.roll(t, LANES - HEAD_DIM // 2, 1)
        rot = jnp.where(first_half, bwd, fwd)
        return t * cosv + rot * sinv

    scale = HEAD_DIM ** -0.5
    qa_off = 0
    ka_off = qa_off + SWA_Q_WIDTH
    va_off = ka_off + SWA_KV_WIDTH
    qb_off = va_off + SWA_KV_WIDTH
    kb_off = qb_off + SB_WIDTH
    vb_off = kb_off + SB_WIDTH
    g_off = vb_off + SB_WIDTH
    t = _dot(xb, w_ref[:, qa_off:qa_off + SWA_Q_WIDTH])
    for c in range(SWA_Q_WIDTH // LANES):
        qa_ref[:, c * LANES:(c + 1) * LANES] = (
            rope(t[:, c * LANES:(c + 1) * LANES]) * (scale * LOG2E)).astype(BF16)
    t = rope(_dot(xb, w_ref[:, ka_off:ka_off + SWA_KV_WIDTH]))
    ka_ref[...] = t.astype(BF16)
    kas_ref[...] = pltpu.roll(t, HEAD_DIM, 1).astype(BF16)
    t = _dot(xb, w_ref[:, va_off:va_off + SWA_KV_WIDTH])
    va_ref[...] = t.astype(BF16)
    vas_ref[...] = pltpu.roll(t, HEAD_DIM, 1).astype(BF16)
    gw = 512
    for c in range(GATE_WIDTH // gw):
        gl = (_dot(xb, w_ref[:, g_off + c * gw:g_off + (c + 1) * gw])
              + bg_ref[:, c * gw:(c + 1) * gw])
        g_ref[:, c * gw:(c + 1) * gw] = jax.nn.sigmoid(gl).astype(BF16)
    qb_ref[...] = (_dot(xb, w_ref[:, qb_off:qb_off + SB_WIDTH]) * (scale * LOG2E)).astype(BF16)
    kb_ref[...] = _dot(xb, w_ref[:, kb_off:kb_off + SB_WIDTH]).astype(BF16)
    vb_ref[...] = _dot(xb, w_ref[:, vb_off:vb_off + SB_WIDTH]).astype(BF16)


def _proj_call(x2, pos2, invf, sign, w_in, b_gate):
    m = x2.shape[0]
    tm = PROJ_TM
    row = lambda w: pl.BlockSpec((tm, w), lambda i: (i, 0))
    full = _resident_spec
    out_widths = [SWA_Q_WIDTH, SWA_KV_WIDTH, SWA_KV_WIDTH, SWA_KV_WIDTH, SWA_KV_WIDTH,
                  SB_WIDTH, SB_WIDTH, SB_WIDTH, GATE_WIDTH]
    return pl.pallas_call(
        _proj_kernel,
        grid=(m // tm,),
        in_specs=[row(D_MODEL), pl.BlockSpec((tm // POS_GROUPS, LANES), lambda i: (i, 0)),
                  full(invf), full(sign), full(w_in), full(b_gate)],
        out_specs=[row(w) for w in out_widths],
        out_shape=[jax.ShapeDtypeStruct((m, w), BF16) for w in out_widths],
        compiler_params=pltpu.CompilerParams(
            dimension_semantics=("arbitrary",), vmem_limit_bytes=VMEM_LIMIT),
        name="proj",
    )(x2, pos2, invf, sign, w_in, b_gate)


def _swa_kernel(sinks_ref, q_ref, k_ref, ks_ref, v_ref, vs_ref, o_ref):
    blk = WINDOW
    lane = lax.broadcasted_iota(jnp.int32, (1, LANES), 1)
    low = lane < HEAD_DIM
    group = SWA_Q_HEADS // SWA_KV_HEADS
    heads = range(SWA_Q_HEADS)
    plain = [(h // group) == (h % 2) for h in heads]
    row_minus_col = (lax.broadcasted_iota(jnp.int32, (blk, 2 * blk), 0)
                     - lax.broadcasted_iota(jnp.int32, (blk, 2 * blk), 1))

    sinks2 = [sinks_ref[h] * LOG2E for h in heads]

    @pl.loop(0, q_ref.shape[0] // (SWA_QBLOCKS * blk))
    def _(i):
        tiles = []
        for u in range(SWA_QBLOCKS):
            qstart = pl.multiple_of((i * SWA_QBLOCKS + u) * blk, blk)
            kstart = pl.multiple_of(jnp.maximum(qstart - blk, 0), blk)
            rel = row_minus_col + (qstart - kstart)
            mask = (rel >= 0) & (rel < WINDOW)
            ksl = pl.ds(kstart, 2 * blk)
            tiles.append((pl.ds(qstart, blk), mask, (k_ref[ksl, :], ks_ref[ksl, :]),
                          (v_ref[ksl, :], vs_ref[ksl, :])))

        work = [(t, h) for t in tiles for h in heads]
        qms = []
        for (qrows, _, _, _), h in work:
            qc = q_ref[qrows, (h // 2) * LANES:(h // 2 + 1) * LANES]
            qms.append(jnp.where(low if h % 2 == 0 else ~low, qc, jnp.zeros_like(qc)))
        ss = [jnp.where(t[1], _dot_nt(qm, t[2][0] if plain[h] else t[2][1]), -1e30)
              for (t, h), qm in zip(work, qms)]
        mxs = [jnp.maximum(jnp.max(s, axis=-1, keepdims=True), sinks2[h])
               for (_, h), s in zip(work, ss)]
        ps = [jnp.exp2(s - mx) for s, mx in zip(ss, mxs)]
        denoms = [jnp.sum(p, axis=-1, keepdims=True) + jnp.exp2(sinks2[h] - mx)
                  for (_, h), p, mx in zip(work, ps, mxs)]
        pvs = [_dot(p.astype(BF16), t[3][0] if plain[h] else t[3][1])
               for (t, h), p in zip(work, ps)]
        outs = [pv / d for pv, d in zip(pvs, denoms)]
        for n in range(0, len(work), 2):
            (qrows, _, _, _), h = work[n]
            o_ref[qrows, (h // 2) * LANES:(h // 2 + 1) * LANES] = jnp.where(
                low, outs[n], outs[n + 1]).astype(BF16)


def _swa_call(sinks, qa, ka, kas, va, vas, batch, seq):
    m = qa.shape[0]
    qspec = pl.BlockSpec((seq, SWA_Q_WIDTH), lambda b: (b, 0))
    kvspec = pl.BlockSpec((seq, SWA_KV_WIDTH), lambda b: (b, 0))
    return pl.pallas_call(
        _swa_kernel,
        grid=(batch,),
        in_specs=[pl.BlockSpec(memory_space=pltpu.SMEM), qspec, kvspec, kvspec, kvspec, kvspec],
        out_specs=qspec,
        out_shape=jax.ShapeDtypeStruct((m, SWA_Q_WIDTH), BF16),
        compiler_params=pltpu.CompilerParams(
            dimension_semantics=("arbitrary",), vmem_limit_bytes=VMEM_LIMIT),
        name="swa",
    )(sinks, qa, ka, kas, va, vas)


def _softplus2(z):
    neg_abs = lax.bitcast_convert_type(
        lax.bitcast_convert_type(z, jnp.uint32) | jnp.uint32(0x80000000), F32)
    return jnp.maximum(z, 0.0) + jnp.log2(1.0 + jnp.exp2(neg_abs))


def _sb_kernel(q_ref, k_ref, v_ref, o_ref, acc_ref, carry_ref, factor_ref):
    tq, tk = SB_TQ, SB_TK
    chunks = q_ref.shape[1] // LANES
    lane = lax.broadcasted_iota(jnp.int32, (1, LANES), 1)
    low = lane < HEAD_DIM
    rr = lax.broadcasted_iota(jnp.int32, (tk, tk), 0)
    cc = lax.broadcasted_iota(jnp.int32, (tk, tk), 1)
    incl_tri = jnp.where(rr >= cc, 1.0, 0.0).astype(BF16)
    heads = range(2 * chunks)
    half = SB_LOWER_ROWS

    def tile(row0, j, nrows, diag):
        qms = []
        for c in range(chunks):
            q = q_ref[pl.ds(row0, nrows), c * LANES:(c + 1) * LANES]
            qms.append(jnp.where(low, q, jnp.zeros_like(q)))
            qms.append(jnp.where(low, jnp.zeros_like(q), q))
        start = pl.multiple_of(j * tk, tk)
        kts = [k_ref[pl.ds(start, tk), c * LANES:(c + 1) * LANES] for c in range(chunks)]
        vts = [v_ref[pl.ds(start, tk), c * LANES:(c + 1) * LANES] for c in range(chunks)]
        if diag:
            causal = (lax.broadcasted_iota(jnp.int32, (nrows, tk), 1)
                      < lax.broadcasted_iota(jnp.int32, (nrows, tk), 0))
        zs = [_dot_nt(qms[h], kts[h // 2]) for h in heads]
        sps = [_softplus2(z) for z in zs]
        if diag:
            sps = [jnp.where(causal, sp, 0.0) for sp in sps]
        incls = [_dot(sp.astype(BF16), incl_tri) for sp in sps]
        aa = [jnp.exp2(z - incl) for z, incl in zip(zs, incls)]
        if diag:
            aa = [jnp.where(causal, a, 0.0) for a in aa]
        pvs = [_dot(aa[h].astype(BF16), vts[h // 2]) for h in heads]
        sums = [jnp.broadcast_to(incl[:, 0:1], (nrows, LANES)) for incl in incls]
        pv = [jnp.where(low, pvs[2 * c], pvs[2 * c + 1]) for c in range(chunks)]
        mass = [jnp.where(low, sums[2 * c], sums[2 * c + 1]) for c in range(chunks)]
        return pv, mass

    def advance(mass, nrows, first):
        fmax = None
        for c in range(chunks):
            sl = slice(c * LANES, (c + 1) * LANES)
            carry = mass[c] if first else carry_ref[0:nrows, sl] + mass[c]
            carry_ref[0:nrows, sl] = carry
            f = jnp.exp2(-carry)
            factor_ref[0:nrows, sl] = f
            fmax = f if fmax is None else jnp.maximum(fmax, f)
        alive_lo = (jnp.max(fmax[0:half]) > 0.0).astype(jnp.int32)
        if nrows == half:
            return alive_lo
        return alive_lo, (jnp.max(fmax[half:]) > 0.0).astype(jnp.int32)

    def update(row0, j, nrows):
        pv, mass = tile(row0, j, nrows, False)
        for c in range(chunks):
            sl = slice(c * LANES, (c + 1) * LANES)
            acc_ref[0:nrows, sl] += factor_ref[0:nrows, sl] * pv[c]
        return advance(mass, nrows, False)

    @pl.loop(0, q_ref.shape[0] // tq)
    def _(i):
        row0 = pl.multiple_of(i * tq, tq)
        pv, mass = tile(row0, i, tq, True)
        for c in range(chunks):
            acc_ref[:, c * LANES:(c + 1) * LANES] = pv[c]
        alive_lo, alive_hi = advance(mass, tq, True)

        def cond(state):
            n, alive_lo, alive_hi = state
            return (n < i) & ((alive_lo | alive_hi) > 0)

        def body(state):
            n, _, alive_hi = state
            j = i - 1 - n
            alive_lo, alive_hi = lax.cond(
                alive_hi > 0,
                lambda: update(row0, j, tq),
                lambda: (update(row0, j, half), jnp.int32(0)))
            return n + 1, alive_lo, alive_hi

        lax.while_loop(cond, body, (jnp.int32(0), alive_lo, alive_hi))
        o_ref[pl.ds(row0, tq), :] = acc_ref[...].astype(BF16)


def _sb_call(qb, kb, vb, batch, seq):
    m = qb.shape[0]
    spec = pl.BlockSpec((seq, SB_WIDTH), lambda b: (b, 0))
    return pl.pallas_call(
        _sb_kernel,
        grid=(batch,),
        in_specs=[spec, spec, spec],
        out_specs=spec,
        out_shape=jax.ShapeDtypeStruct((m, SB_WIDTH), BF16),
        scratch_shapes=[pltpu.VMEM((SB_TQ, SB_WIDTH), F32)] * 3,
        compiler_params=pltpu.CompilerParams(
            dimension_semantics=("arbitrary",), vmem_limit_bytes=VMEM_LIMIT),
        name="stickbreak",
    )(qb, kb, vb)


def _merge_kernel(alpha, x_ref, ya_ref, yb_ref, g_ref, wa_ref, wb_ref, wo_ref, lg_ref, lb_ref, o_ref):
    sub = MERGE_SUB
    rows = [slice(r, r + sub) for r in range(0, MERGE_TM, sub)]
    pas = [_dot(ya_ref[r, :], wa_ref[...]) for r in rows]
    pbs = [_dot(yb_ref[r, :], wb_ref[...]) for r in rows]
    hs = [(g_ref[r, 0:D_MODEL].astype(F32) * pa
           + g_ref[r, D_MODEL:2 * D_MODEL].astype(F32) * pb).astype(BF16)
          for r, pa, pb in zip(rows, pas, pbs)]
    res = [alpha * x_ref[r, :] + _dot(h, wo_ref[...]) for r, h in zip(rows, hs)]
    for r, v in zip(rows, res):
        o_ref[r, :] = _layer_norm(v, lg_ref[...], lb_ref[...])


def _merge_call(alpha, x2, ya, yb, gates, wa, wb, wo, lg, lb):
    m = x2.shape[0]
    tm = MERGE_TM
    row = lambda w: pl.BlockSpec((tm, w), lambda i: (i, 0))
    full = _resident_spec
    return pl.pallas_call(
        functools.partial(_merge_kernel, alpha),
        grid=(m // tm,),
        in_specs=[row(D_MODEL), row(SWA_Q_WIDTH), row(SB_WIDTH), row(GATE_WIDTH),
                  full(wa), full(wb), full(wo), full(lg), full(lb)],
        out_specs=row(D_MODEL),
        out_shape=jax.ShapeDtypeStruct((m, D_MODEL), F32),
        compiler_params=pltpu.CompilerParams(
            dimension_semantics=("arbitrary",), vmem_limit_bytes=VMEM_LIMIT),
        name="merge",
    )(x2, ya, yb, gates, wa, wb, wo, lg, lb)


def _ffn_kernel(alpha, tiles_per_seq, x_ref, halo_ref, wu_ref, cw_ref, cb_ref, wd_ref,
                lg_ref, lb_ref, o_ref, act_ref):
    i = pl.program_id(0)
    tm, hl, cwid = FFN_TM, FFN_HALO, FFN_CW
    x = x_ref[...]
    seq_start = (i % tiles_per_seq) == 0
    halo = jnp.where(seq_start, 0.0, halo_ref[...])
    xe = jnp.concatenate([halo.astype(BF16), x.astype(BF16)], axis=0)

    def conv(u, col):
        sl = slice(col, col + cwid)
        y = (cw_ref[2:3, sl] * u[hl:, :]
             + cw_ref[1:2, sl] * pltpu.roll(u, 1, 0)[hl:, :]
             + cw_ref[0:1, sl] * pltpu.roll(u, 2, 0)[hl:, :])
        return y + cb_ref[:, sl]

    def up_dots(c):
        gcol = c * cwid
        ucol = D_FF + c * cwid
        return _dot(xe, wu_ref[:, gcol:gcol + cwid]), _dot(xe, wu_ref[:, ucol:ucol + cwid])

    nchunks = D_FF // cwid
    ahead = FFN_AHEAD
    pending = [up_dots(c) for c in range(ahead)]
    for c in range(nchunks):
        ug, uu = pending.pop(0)
        if c + ahead < nchunks:
            pending.append(up_dots(c + ahead))
        gcol = c * cwid
        gate = conv(ug, gcol)
        up = conv(uu, D_FF + gcol)
        act_ref[:, gcol:gcol + cwid] = (gate * jax.nn.sigmoid(gate) * up).astype(BF16)
    for r in range(0, tm, FFN_DOWN_ROWS):
        rows = slice(r, r + FFN_DOWN_ROWS)
        f = _dot(act_ref[rows, :], wd_ref[...])
        o_ref[rows, :] = _layer_norm(alpha * x_ref[rows, :] + f, lg_ref[...], lb_ref[...])


def _ffn_call(alpha, x1, wu, cw, cb, wd, lg, lb, seq):
    m = x1.shape[0]
    tm, hl = FFN_TM, FFN_HALO
    full = _resident_spec
    halo_blocks = tm // hl
    return pl.pallas_call(
        functools.partial(_ffn_kernel, alpha, seq // tm),
        grid=(m // tm,),
        in_specs=[pl.BlockSpec((tm, D_MODEL), lambda i: (i, 0)),
                  pl.BlockSpec((hl, D_MODEL), lambda i: (jnp.maximum(i * halo_blocks - 1, 0), 0)),
                  full(wu), full(cw), full(cb), full(wd), full(lg), full(lb)],
        out_specs=pl.BlockSpec((tm, D_MODEL), lambda i: (i, 0)),
        out_shape=jax.ShapeDtypeStruct((m, D_MODEL), F32),
        scratch_shapes=[pltpu.VMEM((tm, D_FF), BF16)],
        compiler_params=pltpu.CompilerParams(
            dimension_semantics=("arbitrary",), vmem_limit_bytes=VMEM_LIMIT),
        name="convffn",
    )(x1, x1, wu, cw, cb, wd, lg, lb)


def kernel(x, positions, w_in, b_gate, sinks, w_branch_a, w_branch_b, w_out, ln1_g, ln1_b,
           w_up, conv_w, conv_b, w_down, ln2_g, ln2_b):
    batch, seq, _ = x.shape
    depth = w_in.shape[0]
    alpha = (2.0 * depth) ** 0.25
    m = batch * seq
    assert seq % SB_TQ == 0 and seq % FFN_TM == 0 and seq % (SWA_QBLOCKS * WINDOW) == 0
    assert m % PROJ_TM == 0 and m % MERGE_TM == 0

    half = jnp.arange(0, HEAD_DIM, 2, dtype=F32) / HEAD_DIM
    inv_freq = 1.0 / (ROPE_THETA ** half)
    invf = jnp.tile(inv_freq, LANES // (HEAD_DIM // 2))[None, :]
    sign = jnp.tile(jnp.concatenate([-jnp.ones(HEAD_DIM // 2, F32), jnp.ones(HEAD_DIM // 2, F32)]),
                    LANES // HEAD_DIM)[None, :]
    slab = PROJ_TM // POS_GROUPS
    pos2 = positions.reshape(m // PROJ_TM, POS_GROUPS, slab).transpose(0, 2, 1)
    pos2 = jnp.repeat(pos2, LANES // POS_GROUPS, axis=2).reshape(m // POS_GROUPS, LANES)
    x2 = x.reshape(m, D_MODEL)

    for l in range(depth):
        qa, ka, kas, va, vas, qb, kb, vb, gates = _proj_call(
            x2, pos2, invf, sign, w_in[l].astype(BF16), b_gate[l][None, :])
        ya = _swa_call(sinks[l], qa, ka, kas, va, vas, batch, seq)
        yb = _sb_call(qb, kb, vb, batch, seq)
        x1 = _merge_call(alpha, x2, ya, yb, gates,
                         w_branch_a[l].astype(BF16), w_branch_b[l].astype(BF16),
                         w_out[l].astype(BF16), ln1_g[l][None, :], ln1_b[l][None, :])
        x2 = _ffn_call(alpha, x1, w_up[l].astype(BF16), conv_w[l], conv_b[l][None, :],
                       w_down[l].astype(BF16), ln2_g[l][None, :], ln2_b[l][None, :], seq)
    return x2.reshape(batch, seq, D_MODEL)
```

```python
import functools
import math

import jax
import jax.numpy as jnp
from jax import lax
from jax.experimental import pallas as pl
from jax.experimental.pallas import tpu as pltpu

D_MODEL = 1024
HEAD_DIM = 64
SWA_Q_HEADS = 8
SWA_KV_HEADS = 2
SB_HEADS = 8
WINDOW = 128
ROPE_THETA = 10000.0
D_FF = 2816
LN_EPS = 1e-5

SWA_Q_WIDTH = SWA_Q_HEADS * HEAD_DIM
SWA_KV_WIDTH = SWA_KV_HEADS * HEAD_DIM
SB_WIDTH = SB_HEADS * HEAD_DIM
GATE_WIDTH = 2 * D_MODEL

LANES = 128
VMEM_LIMIT = 56 * 1024 * 1024

PROJ_TM = 1024
POS_GROUPS = LANES // (HEAD_DIM // 2)
MERGE_TM = 1024
MERGE_SUB = 128
FFN_TM = 1024
FFN_CW = 256
FFN_AHEAD = 2
FFN_DOWN_ROWS = 256
FFN_HALO = 16
SWA_QBLOCKS = 2
SB_TQ = 256
SB_TK = 256

LOG2E = math.log2(math.e)

F32 = jnp.float32
BF16 = jnp.bfloat16


def _dot(a, b):
    return jnp.dot(a, b, preferred_element_type=F32)


def _dot_nt(a, b):
    return lax.dot_general(a, b, (((1,), (1,)), ((), ())), preferred_element_type=F32)


def _resident_spec(a):
    return pl.BlockSpec(a.shape, lambda i: (0, 0), pipeline_mode=pl.Buffered(1))


def _layer_norm(v, g, b):
    mu = jnp.mean(v, axis=-1, keepdims=True)
    vc = v - mu
    var = jnp.mean(vc * vc, axis=-1, keepdims=True)
    return vc * lax.rsqrt(var + LN_EPS) * g + b


def _proj_kernel(x_ref, pos_ref, invf_ref, sign_ref, w_ref, bg_ref,
                 qa_ref, ka_ref, kas_ref, va_ref, vas_ref, qb_ref, kb_ref, vb_ref, g_ref):
    xb = x_ref[...].astype(BF16)
    lane = lax.broadcasted_iota(jnp.int32, (1, LANES), 1)
    first_half = (lane % HEAD_DIM) < (HEAD_DIM // 2)
    nfreq = HEAD_DIM // 2
    ngroups = POS_GROUPS
    ang = pos_ref[...].astype(F32) * invf_ref[...]
    cos_rolled = [jnp.cos(ang)]
    sin_rolled = [jnp.sin(ang)]
    for s in range(1, ngroups):
        cos_rolled.append(pltpu.roll(cos_rolled[0], nfreq * s, 1))
        sin_rolled.append(pltpu.roll(sin_rolled[0], nfreq * s, 1))
    lane_group = lane // nfreq

    def unpack(rolled, g):
        out = rolled[(0 - g) % ngroups]
        for k in range(1, ngroups):
            out = jnp.where(lane_group == k, rolled[(k - g) % ngroups], out)
        return out

    cosv = jnp.concatenate([unpack(cos_rolled, g) for g in range(ngroups)], axis=0)
    sinv = jnp.concatenate([unpack(sin_rolled, g) for g in range(ngroups)], axis=0) * sign_ref[...]

    def rope(t):
        fwd = pltpu.roll(t, HEAD_DIM // 2, 1)
        bwd = pltpu.roll(t, LANES - HEAD_DIM // 2, 1)
        rot = jnp.where(first_half, bwd, fwd)
        return t * cosv + rot * sinv

    scale = HEAD_DIM ** -0.5
    qa_off = 0
    ka_off = qa_off + SWA_Q_WIDTH
    va_off = ka_off + SWA_KV_WIDTH
    qb_off = va_off + SWA_KV_WIDTH
    kb_off = qb_off + SB_WIDTH
    vb_off = kb_off + SB_WIDTH
    g_off = vb_off + SB_WIDTH
    t = _dot(xb, w_ref[:, qa_off:qa_off + SWA_Q_WIDTH])
    for c in range(SWA_Q_WIDTH // LANES):
        qa_ref[:, c * LANES:(c + 1) * LANES] = (
            rope(t[:, c * LANES:(c + 1) * LANES]) * (scale * LOG2E)).astype(BF16)
    t = rope(_dot(xb, w_ref[:, ka_off:ka_off + SWA_KV_WIDTH]))
    ka_ref[...] = t.astype(BF16)
    kas_ref[...] = pltpu.roll(t, HEAD_DIM, 1).astype(BF16)
    t = _dot(xb, w_ref[:, va_off:va_off + SWA_KV_WIDTH])
    va_ref[...] = t.astype(BF16)
    vas_ref[...] = pltpu.roll(t, HEAD_DIM, 1).astype(BF16)
    qb_ref[...] = (_dot(xb, w_ref[:, qb_off:qb_off + SB_WIDTH]) * (scale * LOG2E)).astype(BF16)
    kb_ref[...] = _dot(xb, w_ref[:, kb_off:kb_off + SB_WIDTH]).astype(BF16)
    vb_ref[...] = _dot(xb, w_ref[:, vb_off:vb_off + SB_WIDTH]).astype(BF16)
    gw = 512
    for c in range(GATE_WIDTH // gw):
        gl = (_dot(xb, w_ref[:, g_off + c * gw:g_off + (c + 1) * gw])
              + bg_ref[:, c * gw:(c + 1) * gw])
        g_ref[:, c * gw:(c + 1) * gw] = jax.nn.sigmoid(gl).astype(BF16)


def _proj_call(x2, pos2, invf, sign, w_in, b_gate):
    m = x2.shape[0]
    tm = PROJ_TM
    row = lambda w: pl.BlockSpec((tm, w), lambda i: (i, 0))
    full = _resident_spec
    out_widths = [SWA_Q_WIDTH, SWA_KV_WIDTH, SWA_KV_WIDTH, SWA_KV_WIDTH, SWA_KV_WIDTH,
                  SB_WIDTH, SB_WIDTH, SB_WIDTH, GATE_WIDTH]
    return pl.pallas_call(
        _proj_kernel,
        grid=(m // tm,),
        in_specs=[row(D_MODEL), pl.BlockSpec((tm // POS_GROUPS, LANES), lambda i: (i, 0)),
                  full(invf), full(sign), full(w_in), full(b_gate)],
        out_specs=[row(w) for w in out_widths],
        out_shape=[jax.ShapeDtypeStruct((m, w), BF16) for w in out_widths],
        compiler_params=pltpu.CompilerParams(
            dimension_semantics=("arbitrary",), vmem_limit_bytes=VMEM_LIMIT),
        name="proj",
    )(x2, pos2, invf, sign, w_in, b_gate)


def _swa_kernel(sinks_ref, q_ref, k_ref, ks_ref, v_ref, vs_ref, o_ref):
    blk = WINDOW
    lane = lax.broadcasted_iota(jnp.int32, (1, LANES), 1)
    low = lane < HEAD_DIM
    group = SWA_Q_HEADS // SWA_KV_HEADS
    heads = range(SWA_Q_HEADS)
    plain = [(h // group) == (h % 2) for h in heads]
    row_minus_col = (lax.broadcasted_iota(jnp.int32, (blk, 2 * blk), 0)
                     - lax.broadcasted_iota(jnp.int32, (blk, 2 * blk), 1))

    sinks2 = [sinks_ref[h] * LOG2E for h in heads]

    @pl.loop(0, q_ref.shape[0] // (SWA_QBLOCKS * blk))
    def _(i):
        tiles = []
        for u in range(SWA_QBLOCKS):
            qstart = pl.multiple_of((i * SWA_QBLOCKS + u) * blk, blk)
            kstart = pl.multiple_of(jnp.maximum(qstart - blk, 0), blk)
            rel = row_minus_col + (qstart - kstart)
            mask = (rel >= 0) & (rel < WINDOW)
            ksl = pl.ds(kstart, 2 * blk)
            tiles.append((pl.ds(qstart, blk), mask, (k_ref[ksl, :], ks_ref[ksl, :]),
                          (v_ref[ksl, :], vs_ref[ksl, :])))

        work = [(t, h) for t in tiles for h in heads]
        qms = []
        for (qrows, _, _, _), h in work:
            qc = q_ref[qrows, (h // 2) * LANES:(h // 2 + 1) * LANES]
            qms.append(jnp.where(low if h % 2 == 0 else ~low, qc, jnp.zeros_like(qc)))
        ss = [jnp.where(t[1], _dot_nt(qm, t[2][0] if plain[h] else t[2][1]), -1e30)
              for (t, h), qm in zip(work, qms)]
        mxs = [jnp.maximum(jnp.max(s, axis=-1, keepdims=True), sinks2[h])
               for (_, h), s in zip(work, ss)]
        ps = [jnp.exp2(s - mx) for s, mx in zip(ss, mxs)]
        denoms = [jnp.sum(p, axis=-1, keepdims=True) + jnp.exp2(sinks2[h] - mx)
                  for (_, h), p, mx in zip(work, ps, mxs)]
        pvs = [_dot(p.astype(BF16), t[3][0] if plain[h] else t[3][1])
               for (t, h), p in zip(work, ps)]
        outs = [pv / d for pv, d in zip(pvs, denoms)]
        for n in range(0, len(work), 2):
            (qrows, _, _, _), h = work[n]
            o_ref[qrows, (h // 2) * LANES:(h // 2 + 1) * LANES] = jnp.where(
                low, outs[n], outs[n + 1]).astype(BF16)


def _swa_call(sinks, qa, ka, kas, va, vas, batch, seq):
    m = qa.shape[0]
    qspec = pl.BlockSpec((seq, SWA_Q_WIDTH), lambda b: (b, 0))
    kvspec = pl.BlockSpec((seq, SWA_KV_WIDTH), lambda b: (b, 0))
    return pl.pallas_call(
        _swa_kernel,
        grid=(batch,),
        in_specs=[pl.BlockSpec(memory_space=pltpu.SMEM), qspec, kvspec, kvspec, kvspec, kvspec],
        out_specs=qspec,
        out_shape=jax.ShapeDtypeStruct((m, SWA_Q_WIDTH), BF16),
        compiler_params=pltpu.CompilerParams(
            dimension_semantics=("arbitrary",), vmem_limit_bytes=VMEM_LIMIT),
        name="swa",
    )(sinks, qa, ka, kas, va, vas)


def _softplus2(z):
    neg_abs = lax.bitcast_convert_type(
        lax.bitcast_convert_type(z, jnp.uint32) | jnp.uint32(0x80000000), F32)
    return jnp.maximum(z, 0.0) + jnp.log2(1.0 + jnp.exp2(neg_abs))


def _sb_kernel(q_ref, k_ref, v_ref, o_ref, acc_ref, carry_ref, factor_ref):
    tq, tk = SB_TQ, SB_TK
    chunks = q_ref.shape[1] // LANES
    lane = lax.broadcasted_iota(jnp.int32, (1, LANES), 1)
    low = lane < HEAD_DIM
    rr = lax.broadcasted_iota(jnp.int32, (tk, tk), 0)
    cc = lax.broadcasted_iota(jnp.int32, (tk, tk), 1)
    after_tri = jnp.where(rr > cc, 1.0, 0.0).astype(BF16)
    heads = range(2 * chunks)
    half = tq // 2

    def tile(row0, j, nrows, diag):
        qms = []
        for c in range(chunks):
            q = q_ref[pl.ds(row0, nrows), c * LANES:(c + 1) * LANES]
            qms.append(jnp.where(low, q, jnp.zeros_like(q)))
            qms.append(jnp.where(low, jnp.zeros_like(q), q))
        start = pl.multiple_of(j * tk, tk)
        kts = [k_ref[pl.ds(start, tk), c * LANES:(c + 1) * LANES] for c in range(chunks)]
        vts = [v_ref[pl.ds(start, tk), c * LANES:(c + 1) * LANES] for c in range(chunks)]
        if diag:
            causal = (lax.broadcasted_iota(jnp.int32, (nrows, tk), 1)
                      < lax.broadcasted_iota(jnp.int32, (nrows, tk), 0))
        zs = [_dot_nt(qms[h], kts[h // 2]) for h in heads]
        sps = [_softplus2(z) for z in zs]
        if diag:
            sps = [jnp.where(causal, sp, 0.0) for sp in sps]
        afters = [_dot(sp.astype(BF16), after_tri) for sp in sps]
        aa = [jnp.exp2((z - sp) - after) for z, sp, after in zip(zs, sps, afters)]
        if diag:
            aa = [jnp.where(causal, a, 0.0) for a in aa]
        pvs = [_dot(aa[h].astype(BF16), vts[h // 2]) for h in heads]
        sums = [jnp.broadcast_to(after[:, 0:1] + sp[:, 0:1], (nrows, LANES))
                for sp, after in zip(sps, afters)]
        pv = [jnp.where(low, pvs[2 * c], pvs[2 * c + 1]) for c in range(chunks)]
        mass = [jnp.where(low, sums[2 * c], sums[2 * c + 1]) for c in range(chunks)]
        return pv, mass

    def advance(mass, nrows, first):
        fmax = None
        for c in range(chunks):
            sl = slice(c * LANES, (c + 1) * LANES)
            carry = mass[c] if first else carry_ref[0:nrows, sl] + mass[c]
            carry_ref[0:nrows, sl] = carry
            f = jnp.exp2(-carry)
            factor_ref[0:nrows, sl] = f
            fmax = f if fmax is None else jnp.maximum(fmax, f)
        alive_lo = (jnp.max(fmax[0:half]) > 0.0).astype(jnp.int32)
        if nrows == half:
            return alive_lo
        return alive_lo, (jnp.max(fmax[half:]) > 0.0).astype(jnp.int32)

    def update(row0, j, nrows):
        pv, mass = tile(row0, j, nrows, False)
        for c in range(chunks):
            sl = slice(c * LANES, (c + 1) * LANES)
            acc_ref[0:nrows, sl] += factor_ref[0:nrows, sl] * pv[c]
        return advance(mass, nrows, False)

    @pl.loop(0, q_ref.shape[0] // tq)
    def _(i):
        row0 = pl.multiple_of(i * tq, tq)
        pv, mass = tile(row0, i, tq, True)
        for c in range(chunks):
            acc_ref[:, c * LANES:(c + 1) * LANES] = pv[c]
        alive_lo, alive_hi = advance(mass, tq, True)

        def cond(state):
            n, alive_lo, alive_hi = state
            return (n < i) & ((alive_lo | alive_hi) > 0)

        def body(state):
            n, _, alive_hi = state
            j = i - 1 - n
            alive_lo, alive_hi = lax.cond(
                alive_hi > 0,
                lambda: update(row0, j, tq),
                lambda: (update(row0, j, half), jnp.int32(0)))
            return n + 1, alive_lo, alive_hi

        lax.while_loop(cond, body, (jnp.int32(0), alive_lo, alive_hi))
        o_ref[pl.ds(row0, tq), :] = acc_ref[...].astype(BF16)


def _sb_call(qb, kb, vb, batch, seq):
    m = qb.shape[0]
    spec = pl.BlockSpec((seq, SB_WIDTH), lambda b: (b, 0))
    return pl.pallas_call(
        _sb_kernel,
        grid=(batch,),
        in_specs=[spec, spec, spec],
        out_specs=spec,
        out_shape=jax.ShapeDtypeStruct((m, SB_WIDTH), BF16),
        scratch_shapes=[pltpu.VMEM((SB_TQ, SB_WIDTH), F32)] * 3,
        compiler_params=pltpu.CompilerParams(
            dimension_semantics=("arbitrary",), vmem_limit_bytes=VMEM_LIMIT),
        name="stickbreak",
    )(qb, kb, vb)


def _merge_kernel(alpha, x_ref, ya_ref, yb_ref, g_ref, wa_ref, wb_ref, wo_ref, lg_ref, lb_ref, o_ref):
    sub = MERGE_SUB
    rows = [slice(r, r + sub) for r in range(0, MERGE_TM, sub)]
    pas = [_dot(ya_ref[r, :], wa_ref[...]) for r in rows]
    pbs = [_dot(yb_ref[r, :], wb_ref[...]) for r in rows]
    hs = [(g_ref[r, 0:D_MODEL].astype(F32) * pa
           + g_ref[r, D_MODEL:2 * D_MODEL].astype(F32) * pb).astype(BF16)
          for r, pa, pb in zip(rows, pas, pbs)]
    res = [alpha * x_ref[r, :] + _dot(h, wo_ref[...]) for r, h in zip(rows, hs)]
    for r, v in zip(rows, res):
        o_ref[r, :] = _layer_norm(v, lg_ref[...], lb_ref[...])


def _merge_call(alpha, x2, ya, yb, gates, wa, wb, wo, lg, lb):
    m = x2.shape[0]
    tm = MERGE_TM
    row = lambda w: pl.BlockSpec((tm, w), lambda i: (i, 0))
    full = _resident_spec
    return pl.pallas_call(
        functools.partial(_merge_kernel, alpha),
        grid=(m // tm,),
        in_specs=[row(D_MODEL), row(SWA_Q_WIDTH), row(SB_WIDTH), row(GATE_WIDTH),
                  full(wa), full(wb), full(wo), full(lg), full(lb)],
        out_specs=row(D_MODEL),
        out_shape=jax.ShapeDtypeStruct((m, D_MODEL), F32),
        compiler_params=pltpu.CompilerParams(
            dimension_semantics=("arbitrary",), vmem_limit_bytes=VMEM_LIMIT),
        name="merge",
    )(x2, ya, yb, gates, wa, wb, wo, lg, lb)


def _ffn_kernel(alpha, tiles_per_seq, x_ref, halo_ref, wu_ref, cw_ref, cb_ref, wd_ref,
                lg_ref, lb_ref, o_ref, act_ref):
    i = pl.program_id(0)
    tm, hl, cwid = FFN_TM, FFN_HALO, FFN_CW
    x = x_ref[...]
    seq_start = (i % tiles_per_seq) == 0
    halo = jnp.where(seq_start, 0.0, halo_ref[...])
    xe = jnp.concatenate([halo.astype(BF16), x.astype(BF16)], axis=0)

    def conv(u, col):
        sl = slice(col, col + cwid)
        y = (cw_ref[2:3, sl] * u[hl:, :]
             + cw_ref[1:2, sl] * pltpu.roll(u, 1, 0)[hl:, :]
             + cw_ref[0:1, sl] * pltpu.roll(u, 2, 0)[hl:, :])
        return y + cb_ref[:, sl]

    def up_dots(c):
        gcol = c * cwid
        ucol = D_FF + c * cwid
        return _dot(xe, wu_ref[:, gcol:gcol + cwid]), _dot(xe, wu_ref[:, ucol:ucol + cwid])

    nchunks = D_FF // cwid
    ahead = FFN_AHEAD
    pending = [up_dots(c) for c in range(ahead)]
    for c in range(nchunks):
        ug, uu = pending.pop(0)
        if c + ahead < nchunks:
            pending.append(up_dots(c + ahead))
        gcol = c * cwid
        gate = conv(ug, gcol)
        up = conv(uu, D_FF + gcol)
        act_ref[:, gcol:gcol + cwid] = (gate * jax.nn.sigmoid(gate) * up).astype(BF16)
    for r in range(0, tm, FFN_DOWN_ROWS):
        rows = slice(r, r + FFN_DOWN_ROWS)
        f = _dot(act_ref[rows, :], wd_ref[...])
        o_ref[rows, :] = _layer_norm(alpha * x_ref[rows, :] + f, lg_ref[...], lb_ref[...])


def _ffn_call(alpha, x1, wu, cw, cb, wd, lg, lb, seq):
    m = x1.shape[0]
    tm, hl = FFN_TM, FFN_HALO
    full = _resident_spec
    halo_blocks = tm // hl
    return pl.pallas_call(
        functools.partial(_ffn_kernel, alpha, seq // tm),
        grid=(m // tm,),
        in_specs=[pl.BlockSpec((tm, D_MODEL), lambda i: (i, 0)),
                  pl.BlockSpec((hl, D_MODEL), lambda i: (jnp.maximum(i * halo_blocks - 1, 0), 0)),
                  full(wu), full(cw), full(cb), full(wd), full(lg), full(lb)],
        out_specs=pl.BlockSpec((tm, D_MODEL), lambda i: (i, 0)),
        out_shape=jax.ShapeDtypeStruct((m, D_MODEL), F32),
        scratch_shapes=[pltpu.VMEM((tm, D_FF), BF16)],
        compiler_params=pltpu.CompilerParams(
            dimension_semantics=("arbitrary",), vmem_limit_bytes=VMEM_LIMIT),
        name="convffn",
    )(x1, x1, wu, cw, cb, wd, lg, lb)


def kernel(x, positions, w_in, b_gate, sinks, w_branch_a, w_branch_b, w_out, ln1_g, ln1_b,
           w_up, conv_w, conv_b, w_down, ln2_g, ln2_b):
    batch, seq, _ = x.shape
    depth = w_in.shape[0]
    alpha = (2.0 * depth) ** 0.25
    m = batch * seq
    assert seq % SB_TQ == 0 and seq % FFN_TM == 0 and seq % (SWA_QBLOCKS * WINDOW) == 0
    assert m % PROJ_TM == 0 and m % MERGE_TM == 0

    half = jnp.arange(0, HEAD_DIM, 2, dtype=F32) / HEAD_DIM
    inv_freq = 1.0 / (ROPE_THETA ** half)
    invf = jnp.tile(inv_freq, LANES // (HEAD_DIM // 2))[None, :]
    sign = jnp.tile(jnp.concatenate([-jnp.ones(HEAD_DIM // 2, F32), jnp.ones(HEAD_DIM // 2, F32)]),
                    LANES // HEAD_DIM)[None, :]
    slab = PROJ_TM // POS_GROUPS
    pos2 = positions.reshape(m // PROJ_TM, POS_GROUPS, slab).transpose(0, 2, 1)
    pos2 = jnp.repeat(pos2, LANES // POS_GROUPS, axis=2).reshape(m // POS_GROUPS, LANES)
    x2 = x.reshape(m, D_MODEL)

    for l in range(depth):
        qa, ka, kas, va, vas, qb, kb, vb, gates = _proj_call(
            x2, pos2, invf, sign, w_in[l].astype(BF16), b_gate[l][None, :])
        ya = _swa_call(sinks[l], qa, ka, kas, va, vas, batch, seq)
        yb = _sb_call(qb, kb, vb, batch, seq)
        x1 = _merge_call(alpha, x2, ya, yb, gates,
                         w_branch_a[l].astype(BF16), w_branch_b[l].astype(BF16),
                         w_out[l].astype(BF16), ln1_g[l][None, :], ln1_b[l][None, :])
        x2 = _ffn_call(alpha, x1, w_up[l].astype(BF16), conv_w[l], conv_b[l][None, :],
                       w_down[l].astype(BF16), ln2_g[l][None, :], ln2_b[l][None, :], seq)
    return x2.reshape(batch, seq, D_MODEL)
```
